```python
import jax, jax.numpy as jnp
from jax import lax
import numpy as np

D_MODEL = 2048
BATCH = 1
SEQ = 16384
DEPTH = 4
DEC_BATCH = 16
DEC_SEQ = 2048
PAST_LEN = 128

GRID_W = 64
N_MIXERS = 3
EXPAND = 2
D_INNER = EXPAND * D_MODEL
RMS_EPS = 1e-6

FNET_GROUPS = 8
FNET_GROUP_W = D_INNER // FNET_GROUPS

NAT_HEADS = 32
NAT_HEAD_DIM = D_INNER // NAT_HEADS
NAT_WIN_H = 8
NAT_WIN_W = 16
NAT_COL_BLOCK = 16
NAT_KEY_COLS = 32
N_COL_BLOCKS = GRID_W // NAT_COL_BLOCK

GLA_HEADS = 4
GLA_KEY_DIM = D_MODEL // 2
GLA_HEAD_K = GLA_KEY_DIM // GLA_HEADS
GLA_HEAD_V = D_INNER // GLA_HEADS
GLA_GATE_RANK = 16
GLA_GATE_TEMP = 16.0
GLA_CHUNK = 64

N_FNET_LAYERS = (DEPTH + 2) // 3
N_NAT_LAYERS = (DEPTH + 1) // 3
N_GLA_LAYERS = DEPTH // 3

kernel_name = "hybrid_fnet_nat_gla_encoder"


def _rmsnorm(x, g):
    xf = x.astype(jnp.float32)
    y = xf * lax.rsqrt(jnp.mean(xf * xf, axis=-1, keepdims=True) + RMS_EPS)
    return (y * g.astype(jnp.float32)).astype(x.dtype)


def _fnet_mixer(h, w_in, w_out):
    B, S, _ = h.shape
    u, z = jnp.split(h @ w_in, 2, axis=-1)
    ug = u.astype(jnp.float32).reshape(B, S, FNET_GROUPS, FNET_GROUP_W)
    mixed = jnp.fft.fft2(ug, axes=(1, 3), norm="ortho").real
    mixed = mixed.astype(h.dtype).reshape(B, S, D_INNER)
    return (mixed * jax.nn.silu(z)) @ w_out


def _nat_col_tables():
    q_cols = np.arange(GRID_W)
    win_start = np.clip(q_cols - NAT_WIN_W // 2, 0, GRID_W - NAT_WIN_W)
    band_start = np.clip(np.arange(N_COL_BLOCKS) * NAT_COL_BLOCK - NAT_WIN_W // 2,
                         0, GRID_W - NAT_KEY_COLS)
    qc = q_cols.reshape(N_COL_BLOCKS, NAT_COL_BLOCK)
    ws = win_start.reshape(N_COL_BLOCKS, NAT_COL_BLOCK)
    kc = band_start[:, None] + np.arange(NAT_KEY_COLS)[None]
    valid = (kc[:, None, :] >= ws[:, :, None]) & (kc[:, None, :] < ws[:, :, None] + NAT_WIN_W)
    rel = np.clip(kc[:, None, :] - qc[:, :, None], -(NAT_WIN_W - 1), NAT_WIN_W - 1) + NAT_WIN_W - 1
    return band_start, valid, rel


def _nat_mixer(h, w_in, rpb, w_out):
    B, S, _ = h.shape
    rows = S // GRID_W
    kh = min(NAT_WIN_H, rows)
    q, k, v, z = jnp.split(h @ w_in, 4, axis=-1)
    grid = lambda t: t.reshape(B, rows, GRID_W, NAT_HEADS, NAT_HEAD_DIM)
    q = grid(q) * (NAT_HEAD_DIM ** -0.5)
    k = grid(k)
    v = grid(v)
    band_start, valid, rel = _nat_col_tables()
    valid = jnp.asarray(valid)
    rpb_cols = rpb[:, :, jnp.asarray(rel)]

    def row_step(r):
        rs = jnp.clip(r - kh // 2, 0, rows - kh)
        k_band = lax.dynamic_slice_in_dim(k, rs, kh, axis=1)
        v_band = lax.dynamic_slice_in_dim(v, rs, kh, axis=1)
        q_row = lax.dynamic_index_in_dim(q, r, axis=1, keepdims=False)
        row_off = rs + jnp.arange(kh) - r + NAT_WIN_H - 1
        bias = rpb_cols[:, row_off]
        outs = []
        for cb in range(N_COL_BLOCKS):
            c0 = int(band_start[cb])
            kb = k_band[:, :, c0:c0 + NAT_KEY_COLS]
            vb = v_band[:, :, c0:c0 + NAT_KEY_COLS]
            qb = q_row[:, cb * NAT_COL_BLOCK:(cb + 1) * NAT_COL_BLOCK]
            s = jnp.einsum('bqhd,brkhd->bhqrk', qb, kb).astype(jnp.float32)
            s = s + jnp.transpose(bias[:, :, cb], (0, 2, 1, 3)).astype(jnp.float32)[None]
            s = jnp.where(valid[cb][None, None, :, None, :], s, -1e30)
            p = jax.nn.softmax(s.reshape(B, NAT_HEADS, NAT_COL_BLOCK, kh * NAT_KEY_COLS), axis=-1)
            p = p.reshape(B, NAT_HEADS, NAT_COL_BLOCK, kh, NAT_KEY_COLS).astype(vb.dtype)
            outs.append(jnp.einsum('bhqrk,brkhd->bqhd', p, vb))
        return jnp.concatenate(outs, axis=1)

    o = lax.map(row_step, jnp.arange(rows))
    o = jnp.transpose(o, (1, 0, 2, 3, 4)).reshape(B, S, D_INNER)
    return (o * jax.nn.silu(z)) @ w_out


def _gla_direction(q, k, v, log_a, strict):
    B, S, H, dk = q.shape
    dv = v.shape[-1]
    C = GLA_CHUNK
    n = S // C
    chunk = lambda t: t.reshape(B, n, C, H, t.shape[-1])
    q, k, v, log_a = chunk(q), chunk(k), chunk(v), chunk(log_a)
    b = jnp.cumsum(log_a, axis=2)
    b_last = b[:, :, -1]
    b_mid = b[:, :, C // 2 - 1:C // 2]
    scores = jnp.einsum('bnihk,bnjhk->bnhij', q * jnp.exp(b - b_mid), k * jnp.exp(b_mid - b))
    mask = jnp.asarray(np.tril(np.ones((C, C), dtype=bool), k=-1 if strict else 0))
    scores = jnp.where(mask, scores, 0.0)
    o_intra = jnp.einsum('bnhij,bnjhv->bnihv', scores, v)
    q_in = q * jnp.exp(b)
    k_out = k * jnp.exp(b_last[:, :, None] - b)

    def step(state, xs):
        q_c, k_c, v_c, bl = xs
        o = jnp.einsum('bihk,bhkv->bihv', q_c, state)
        state = jnp.exp(bl)[..., None] * state + jnp.einsum('bjhk,bjhv->bhkv', k_c, v_c)
        return state, o

    state0 = jnp.zeros((B, H, dk, dv), jnp.float32)
    xs = (jnp.moveaxis(q_in, 1, 0), jnp.moveaxis(k_out, 1, 0), jnp.moveaxis(v, 1, 0), jnp.moveaxis(b_last, 1, 0))
    _, o_inter = lax.scan(step, state0, xs)
    return (o_intra + jnp.moveaxis(o_inter, 0, 1)).reshape(B, S, H, dv)


def _gla_mixer(h, w_in, wa1_f, wa2_f, ba_f, wa1_b, wa2_b, ba_b, g_norm, w_out):
    B, S, _ = h.shape
    q, k, v, z = jnp.split(h @ w_in, [GLA_KEY_DIM, 2 * GLA_KEY_DIM, 2 * GLA_KEY_DIM + D_INNER], axis=-1)
    heads = lambda t, d: t.astype(jnp.float32).reshape(B, S, GLA_HEADS, d)
    q = heads(q, GLA_HEAD_K) * (GLA_HEAD_K ** -0.5)
    k = heads(k, GLA_HEAD_K)
    v = heads(v, GLA_HEAD_V)

    def log_gate(wa1, wa2, ba):
        pre = ((h @ wa1) @ wa2 + ba).astype(jnp.float32)
        return heads(jax.nn.log_sigmoid(pre) / GLA_GATE_TEMP, GLA_HEAD_K)

    rev = lambda t: jnp.flip(t, axis=1)
    o_f = _gla_direction(q, k, v, log_gate(wa1_f, wa2_f, ba_f), strict=False)
    o_b = rev(_gla_direction(rev(q), rev(k), rev(v), rev(log_gate(wa1_b, wa2_b, ba_b)), strict=True))
    o = o_f + o_b
    o = o * lax.rsqrt(jnp.mean(o * o, axis=-1, keepdims=True) + RMS_EPS) * g_norm.astype(jnp.float32)
    o = o.reshape(B, S, D_INNER).astype(h.dtype)
    return (o * jax.nn.silu(z)) @ w_out


def _trunk(x, norm_pre_g, norm_post_g, fnet_w_in, fnet_w_out, nat_w_in, nat_rpb, nat_w_out,
           gla_w_in, gla_wa1_f, gla_wa2_f, gla_ba_f, gla_wa1_b, gla_wa2_b, gla_ba_b, gla_g_norm, gla_w_out):
    for i in range(DEPTH):
        h = _rmsnorm(x, norm_pre_g[i])
        m, j = i % N_MIXERS, i // N_MIXERS
        if m == 0:
            y = _fnet_mixer(h, fnet_w_in[j], fnet_w_out[j])
        elif m == 1:
            y = _nat_mixer(h, nat_w_in[j], nat_rpb[j], nat_w_out[j])
        else:
            y = _gla_mixer(h, gla_w_in[j], gla_wa1_f[j], gla_wa2_f[j], gla_ba_f[j],
                           gla_wa1_b[j], gla_wa2_b[j], gla_ba_b[j], gla_g_norm[j], gla_w_out[j])
        x = x + _rmsnorm(y, norm_post_g[i])
    return x


def setup_inputs(seed: int = 0) -> dict:
    key = jax.random.key(seed)
    ks = jax.random.split(key, 20)
    nrm = lambda k, shape, scale: jax.random.normal(k, shape, jnp.float32) * scale
    D, E = D_MODEL, D_INNER
    return {
        "x_prompt": nrm(ks[0], (BATCH, SEQ, D), 1.0),
        "x_sample": nrm(ks[1], (DEC_BATCH, DEC_SEQ, D), 1.0),
        "norm_pre_g": 1.0 + nrm(ks[2], (DEPTH, D), 0.02),
        "norm_post_g": 1.0 + nrm(ks[3], (DEPTH, D), 0.02),
        "fnet_w_in": nrm(ks[4], (N_FNET_LAYERS, D, 2 * E), D ** -0.5),
        "fnet_w_out": nrm(ks[5], (N_FNET_LAYERS, E, D), E ** -0.5),
        "nat_w_in": nrm(ks[6], (N_NAT_LAYERS, D, 4 * E), D ** -0.5),
        "nat_rpb": nrm(ks[7], (N_NAT_LAYERS, NAT_HEADS, 2 * NAT_WIN_H - 1, 2 * NAT_WIN_W - 1), 0.1),
        "nat_w_out": nrm(ks[8], (N_NAT_LAYERS, E, D), E ** -0.5),
        "gla_w_in": nrm(ks[9], (N_GLA_LAYERS, D, 2 * GLA_KEY_DIM + 2 * E), D ** -0.5),
        "gla_wa1_f": nrm(ks[10], (N_GLA_LAYERS, D, GLA_GATE_RANK), D ** -0.5),
        "gla_wa2_f": nrm(ks[11], (N_GLA_LAYERS, GLA_GATE_RANK, GLA_KEY_DIM), GLA_GATE_RANK ** -0.5),
        "gla_ba_f": nrm(ks[12], (N_GLA_LAYERS, GLA_KEY_DIM), 0.1),
        "gla_wa1_b": nrm(ks[13], (N_GLA_LAYERS, D, GLA_GATE_RANK), D ** -0.5),
        "gla_wa2_b": nrm(ks[14], (N_GLA_LAYERS, GLA_GATE_RANK, GLA_KEY_DIM), GLA_GATE_RANK ** -0.5),
        "gla_ba_b": nrm(ks[15], (N_GLA_LAYERS, GLA_KEY_DIM), 0.1),
        "gla_g_norm": 1.0 + nrm(ks[16], (N_GLA_LAYERS, GLA_HEAD_V), 0.02),
        "gla_w_out": nrm(ks[17], (N_GLA_LAYERS, E, D), E ** -0.5),
    }


def reference(x_prompt, x_sample, norm_pre_g, norm_post_g, fnet_w_in, fnet_w_out, nat_w_in, nat_rpb,
              nat_w_out, gla_w_in, gla_wa1_f, gla_wa2_f, gla_ba_f, gla_wa1_b, gla_wa2_b, gla_ba_b,
              gla_g_norm, gla_w_out):
    y_prompt = _trunk(x_prompt, norm_pre_g, norm_post_g, fnet_w_in, fnet_w_out, nat_w_in, nat_rpb, nat_w_out,
                      gla_w_in, gla_wa1_f, gla_wa2_f, gla_ba_f, gla_wa1_b, gla_wa2_b, gla_ba_b, gla_g_norm, gla_w_out)
    y_sample = _trunk(x_sample, norm_pre_g, norm_post_g, fnet_w_in, fnet_w_out, nat_w_in, nat_rpb, nat_w_out,
                      gla_w_in, gla_wa1_f, gla_wa2_f, gla_ba_f, gla_wa1_b, gla_wa2_b, gla_ba_b, gla_g_norm, gla_w_out)
    return (y_prompt, y_sample)
```

```python
import functools
import math

import numpy as np
import jax
import jax.numpy as jnp
from jax import lax
from jax.experimental import pallas as pl
from jax.experimental.pallas import tpu as pltpu

F32 = jnp.float32
BF16 = jnp.bfloat16

RMS_EPS = 1e-6
GRID_W = 64
FNET_GROUP_W = 512
DFT_N2 = 128
NAT_HEAD_DIM = 128
NAT_WIN_H = 8
NAT_WIN_W = 16
NAT_Q_ROWS = 8
NAT_KEY_SUB = 4
NAT_N_SUB = 4
GLA_HEADS = 4
GLA_GATE_RANK = 16
GLA_GATE_TEMP = 16.0
GLA_CHUNK = 64
NEG_MASK = -1e30

V7X_VMEM_LIMIT_BYTES = 58 * 1024 * 1024


def _params(*sem):
    return pltpu.CompilerParams(dimension_semantics=sem, vmem_limit_bytes=V7X_VMEM_LIMIT_BYTES)


def _silu(z):
    return z / (1.0 + jnp.exp(-z))


def _dot(a, b):
    return jnp.dot(a, b, preferred_element_type=F32)


def _dot_nt(a, b):
    return lax.dot_general(a, b, (((1,), (1,)), ((), ())), preferred_element_type=F32)


def _dot_tn(a, b):
    return lax.dot_general(a, b, (((0,), (0,)), ((), ())), preferred_element_type=F32)


def _split_bf16(a):
    hi = a.astype(BF16)
    lo = (a - hi.astype(F32)).astype(BF16)
    return hi, lo


def _in_proj_kernel(*refs, has_aux, row_chunk):
    if has_aux:
        x_ref, g_ref, w_ref, wa_ref, o_ref, aux_ref, h_ref = refs
    else:
        x_ref, g_ref, w_ref, o_ref, h_ref = refs

    @pl.when(pl.program_id(1) == 0)
    def _():
        def body(c, carry):
            r = pl.multiple_of(c * row_chunk, row_chunk)
            x = x_ref[pl.ds(r, row_chunk), :]
            ms = jnp.mean(x * x, axis=-1, keepdims=True)
            hn = x * lax.rsqrt(ms + RMS_EPS) * g_ref[...]
            h_ref[pl.ds(r, row_chunk), :] = hn.astype(BF16)
            if has_aux:
                h_hi, h_lo = _split_bf16(hn)
                w_hi = wa_ref[0]
                w_lo = wa_ref[1]
                aux_ref[pl.ds(r, row_chunk), :] = _dot(h_hi, w_hi) + _dot(h_lo, w_hi) + _dot(h_hi, w_lo)
            return carry
        lax.fori_loop(0, x_ref.shape[0] // row_chunk, body, 0)

    o_ref[...] = _dot(h_ref[...], w_ref[...]).astype(o_ref.dtype)


def _in_proj(x2d, g, w_bf16, w_aux=None):
    t, d = x2d.shape
    n = w_bf16.shape[1]
    tm = min(1024, t)
    tn = min(1024, n)
    assert t % tm == 0 and n % tn == 0 and tm % 128 == 0
    has_aux = w_aux is not None
    in_specs = [
        pl.BlockSpec((tm, d), lambda i, j: (i, 0)),
        pl.BlockSpec((1, d), lambda i, j: (0, 0)),
        pl.BlockSpec((d, tn), lambda i, j: (0, j)),
    ]
    args = [x2d, g.reshape(1, d), w_bf16]
    out_shape = [jax.ShapeDtypeStruct((t, n), BF16)]
    out_specs = [pl.BlockSpec((tm, tn), lambda i, j: (i, j))]
    if has_aux:
        na = w_aux.shape[1]
        hi, lo = _split_bf16(w_aux)
        in_specs.append(pl.BlockSpec((2, d, na), lambda i, j: (0, 0, 0)))
        args.append(jnp.stack([hi, lo]))
        out_shape.append(jax.ShapeDtypeStruct((t, na), F32))
        out_specs.append(pl.BlockSpec((tm, na), lambda i, j: (i, 0)))
    res = pl.pallas_call(
        functools.partial(_in_proj_kernel, has_aux=has_aux, row_chunk=128),
        grid=(t // tm, n // tn),
        in_specs=in_specs,
        out_specs=out_specs,
        out_shape=out_shape,
        scratch_shapes=[pltpu.VMEM((tm, d), BF16)],
        compiler_params=_params("parallel", "arbitrary"),
    )(*args)
    return res if has_aux else res[0]


def _out_proj_kernel(a_ref, w_ref, g_ref, x_ref, o_ref, *, rb):
    t = _dot(a_ref[...], w_ref[...])
    ms = jnp.mean(t * t, axis=-1, keepdims=True)
    yn = t * lax.rsqrt(ms + RMS_EPS) * g_ref[...]
    tr = a_ref.shape[0] // rb
    d = w_ref.shape[1]
    for j in range(rb):
        o_ref[:, j * d:(j + 1) * d] = x_ref[:, j * d:(j + 1) * d] + yn[j * tr:(j + 1) * tr]


def _out_proj(a2d, w_bf16, g, x, *, transposed_n1=None):
    t, e = a2d.shape
    d = w_bf16.shape[1]
    w_spec = pl.BlockSpec((e, d), lambda i: (0, 0), pipeline_mode=pl.Buffered(1))
    g_spec = pl.BlockSpec((1, d), lambda i: (0, 0))
    if transposed_n1 is None:
        tm = min(512, t)
        assert t % tm == 0
        rb = 1
        xv = x
        x_spec = pl.BlockSpec((tm, d), lambda i: (i, 0))
        grid = (t // tm,)
    else:
        n1 = transposed_n1
        b = x.shape[0]
        rb = min(4, n1)
        assert n1 % rb == 0
        tm = rb * DFT_N2
        xv = x.reshape(b, DFT_N2, n1 * d)
        nk = n1 // rb
        x_spec = pl.BlockSpec((None, DFT_N2, rb * d), lambda i: (i // nk, 0, i % nk))
        grid = (t // tm,)
    out = pl.pallas_call(
        functools.partial(_out_proj_kernel, rb=rb),
        grid=grid,
        in_specs=[pl.BlockSpec((tm, e), lambda i: (i, 0)), w_spec, g_spec, x_spec],
        out_specs=x_spec,
        out_shape=jax.ShapeDtypeStruct(xv.shape, F32),
        compiler_params=_params("parallel"),
    )(a2d, w_bf16, g.reshape(1, d), xv)
    return out.reshape(x.shape)


def _dft_tables(n1, gb):
    n2 = DFT_N2
    n = n1 * n2
    k1 = np.arange(n1)
    ang_a = 2.0 * np.pi * np.outer(k1, k1) / n1
    eye = np.eye(gb)
    fa_re = np.kron(eye, np.cos(ang_a))
    fa_im = np.kron(eye, -np.sin(ang_a))
    fa = np.stack([fa_re, fa_im], axis=1).reshape(2 * gb * n1, gb * n1) / math.sqrt(n1)

    kk1 = jnp.arange(n1, dtype=jnp.int32)[:, None, None]
    kk2 = jnp.arange(n2, dtype=jnp.int32)[None, :, None]
    nn2 = jnp.arange(n2, dtype=jnp.int32)[None, None, :]
    m = (nn2 * (kk1 + n1 * kk2)) % n
    ang = m.astype(F32) * (2.0 * math.pi / n)
    c = jnp.cos(ang) / math.sqrt(n2)
    s = jnp.sin(ang) / math.sqrt(n2)
    mtab = jnp.concatenate([jnp.concatenate([c, s], axis=2), jnp.concatenate([-s, c], axis=2)], axis=1)

    cw = FNET_GROUP_W
    ang_c = 2.0 * np.pi * (np.outer(np.arange(cw), np.arange(cw)) % cw) / cw
    cc = np.cos(ang_c) / math.sqrt(cw)
    sc = np.sin(ang_c) / math.sqrt(cw)
    return (jnp.asarray(fa, F32).astype(BF16), mtab.astype(BF16),
            jnp.asarray(cc, F32).astype(BF16), jnp.asarray(sc, F32).astype(BF16))


def _dft_a_kernel(f_ref, u_ref, o_ref):
    o_ref[...] = _dot(f_ref[...], u_ref[...]).astype(o_ref.dtype)


def _dft_c_kernel(*refs, rb, cb):
    a_ref, m_ref, cc_ref, sc_ref = refs[:4]
    z_refs = refs[4:4 + rb]
    o_ref, zr_ref, zi_ref = refs[4 + rb:]
    n2 = DFT_N2
    for j in range(rb):
        zz = _dot(m_ref[j], a_ref[j])
        zr_ref[j * n2:(j + 1) * n2, :] = zz[:n2].astype(BF16)
        zi_ref[j * n2:(j + 1) * n2, :] = zz[n2:].astype(BF16)
    cw = FNET_GROUP_W
    for gi in range(cb // cw):
        sl = slice(gi * cw, (gi + 1) * cw)
        y = _dot(zr_ref[:, sl], cc_ref[...]) + _dot(zi_ref[:, sl], sc_ref[...])
        for j in range(rb):
            z = z_refs[j][:, sl].astype(F32)
            o_ref[j * n2:(j + 1) * n2, sl] = (y[j * n2:(j + 1) * n2] * _silu(z)).astype(o_ref.dtype)


def _fnet_mixer(uz, b, s):
    e = uz.shape[1] // 2
    n2 = DFT_N2
    assert s % n2 == 0
    n1 = s // n2
    gb = max(1, min(b, 256 // n1))
    while b % gb:
        gb -= 1
    rg = gb * n1
    fa, mtab, cc, sc = _dft_tables(n1, gb)

    ca = min(4096, e)
    u3 = uz.reshape(b // gb, rg, n2 * 2 * e)
    npc = e // ca
    a = pl.pallas_call(
        _dft_a_kernel,
        grid=(b // gb, n2 * npc),
        in_specs=[pl.BlockSpec((2 * rg, rg), lambda g, c: (0, 0)),
                  pl.BlockSpec((None, rg, ca), lambda g, c: (g, 0, (c // npc) * 2 * npc + c % npc))],
        out_specs=pl.BlockSpec((None, 2 * rg, ca), lambda g, c: (g, 0, c)),
        out_shape=jax.ShapeDtypeStruct((b // gb, 2 * rg, n2 * e), BF16),
        compiler_params=_params("parallel", "parallel"),
    )(fa, u3)

    r = b * n1
    a3 = a.reshape(r, 2 * n2, e)
    rb = min(8, n1)
    assert n1 % rb == 0
    cb = min(1024, e)
    ncb = e // cb
    zv = uz.reshape(b, n2, n1 * 2 * e)
    nkb = n1 // rb

    def z_spec(j):
        def imap(i, c):
            rr = i * rb + j
            return (rr // n1, 0, (rr % n1) * 2 * ncb + ncb + c)
        return pl.BlockSpec((None, n2, cb), imap)

    out = pl.pallas_call(
        functools.partial(_dft_c_kernel, rb=rb, cb=cb),
        grid=(r // rb, ncb),
        in_specs=[pl.BlockSpec((rb, 2 * n2, cb), lambda i, c: (i, 0, c)),
                  pl.BlockSpec((rb, 2 * n2, 2 * n2), lambda i, c: (i % nkb, 0, 0)),
                  pl.BlockSpec((FNET_GROUP_W, FNET_GROUP_W), lambda i, c: (0, 0)),
                  pl.BlockSpec((FNET_GROUP_W, FNET_GROUP_W), lambda i, c: (0, 0))]
                 + [z_spec(j) for j in range(rb)],
        out_specs=pl.BlockSpec((rb * n2, cb), lambda i, c: (i, c)),
        out_shape=jax.ShapeDtypeStruct((r * n2, e), BF16),
        scratch_shapes=[pltpu.VMEM((rb * n2, cb), BF16), pltpu.VMEM((rb * n2, cb), BF16)],
        compiler_params=_params("parallel", "arbitrary"),
    )(a3, mtab, cc, sc, *([zv] * rb))
    return out, n1


def _nat_bias_pairs(rpb):
    qc = np.arange(GRID_W)[:, None]
    kc = np.arange(GRID_W)[None, :]
    ws = np.clip(qc - NAT_WIN_W // 2, 0, GRID_W - NAT_WIN_W)
    valid = (kc >= ws) & (kc < ws + NAT_WIN_W)
    rel = np.clip(kc - qc, -(NAT_WIN_W - 1), NAT_WIN_W - 1) + NAT_WIN_W - 1
    colb = jnp.where(jnp.asarray(valid)[None, None], rpb[:, :, jnp.asarray(rel)].astype(F32), NEG_MASK)
    return jnp.concatenate([colb[:, :-1], colb[:, 1:]], axis=-1)


def _nat_kernel(*refs, n_rb, heads):
    q_ref = refs[0]
    k_refs = refs[1:1 + NAT_N_SUB]
    v_refs = refs[1 + NAT_N_SUB:1 + 2 * NAT_N_SUB]
    z_ref, bias_ref, o_ref, s_ref, p_ref = refs[1 + 2 * NAT_N_SUB:]
    hd = NAT_HEAD_DIM
    w = GRID_W
    nq = NAT_Q_ROWS * w
    nks = NAT_KEY_SUB * w
    scale = hd ** -0.5
    lane = lax.broadcasted_iota(jnp.int32, (w, 2 * w), 1)

    def run(window_starts):
        def head_body(h, carry):
            off = pl.multiple_of(h * hd, hd)
            q = q_ref[:, :, pl.ds(off, hd)].reshape(nq, hd)
            for j in range(NAT_N_SUB):
                kj = k_refs[j][:, :, pl.ds(off, hd)].reshape(nks, hd)
                s_ref[:, j * nks:(j + 1) * nks] = _dot_nt(q, kj)
            for i in range(NAT_Q_ROWS):
                lo = window_starts[i]
                kp_lo, kp_hi = lo // 2, (lo + NAT_WIN_H - 1) // 2
                rows = slice(i * w, (i + 1) * w)
                tiles = []
                for kp in range(kp_lo, kp_hi + 1):
                    t = s_ref[rows, kp * 2 * w:(kp + 1) * 2 * w] * scale + bias_ref[h, 2 * kp - i + 3]
                    if 2 * kp < lo:
                        t = jnp.where(lane >= w, t, NEG_MASK)
                    if 2 * kp + 1 >= lo + NAT_WIN_H:
                        t = jnp.where(lane < w, t, NEG_MASK)
                    tiles.append(t)
                m = tiles[0]
                for t in tiles[1:]:
                    m = jnp.maximum(m, t)
                m = jnp.max(m, axis=-1, keepdims=True)
                es = [jnp.exp(t - m) for t in tiles]
                tot = es[0]
                for ee in es[1:]:
                    tot = tot + ee
                inv = 1.0 / jnp.sum(tot, axis=-1, keepdims=True)
                for kp in range(NAT_N_SUB * NAT_KEY_SUB // 2):
                    cols = slice(kp * 2 * w, (kp + 1) * 2 * w)
                    if kp_lo <= kp <= kp_hi:
                        p_ref[rows, cols] = (es[kp - kp_lo] * inv).astype(BF16)
                    else:
                        p_ref[rows, cols] = jnp.zeros((w, 2 * w), BF16)
            acc = None
            for j in range(NAT_N_SUB):
                vj = v_refs[j][:, :, pl.ds(off, hd)].reshape(nks, hd)
                part = _dot(p_ref[:, j * nks:(j + 1) * nks], vj)
                acc = part if acc is None else acc + part
            z = z_ref[:, :, pl.ds(off, hd)].reshape(nq, hd).astype(F32)
            o_ref[:, :, pl.ds(off, hd)] = (acc * _silu(z)).astype(o_ref.dtype).reshape(NAT_Q_ROWS, w, hd)
            return carry
        lax.fori_loop(0, heads, head_body, 0)

    rb = pl.program_id(2)
    half = NAT_WIN_H // 2
    interior = list(range(NAT_Q_ROWS))
    top = [max(i, half) for i in range(NAT_Q_ROWS)]
    bottom = [min(i, half) for i in range(NAT_Q_ROWS)]

    @pl.when(rb == 0)
    def _():
        run(top)

    @pl.when(rb == n_rb - 1)
    def _():
        run(bottom)

    @pl.when(jnp.logical_and(rb > 0, rb < n_rb - 1))
    def _():
        run(interior)


def _nat_mixer(qkvz, rpb, b, s):
    e = qkvz.shape[1] // 4
    w = GRID_W
    rows = s // w
    assert s % w == 0 and rows % NAT_Q_ROWS == 0 and rows >= 2 * NAT_Q_ROWS
    n_rb = rows // NAT_Q_ROWS
    hb = 8
    lw = hb * NAT_HEAD_DIM
    nhg = e // lw
    x4 = qkvz.reshape(b, rows, w, 4 * e)
    bias = _nat_bias_pairs(rpb)
    n_kblk = rows // NAT_KEY_SUB

    def kv_spec(j, sec):
        def imap(g, bi, r):
            blk = jnp.clip(2 * r - 1 + j, 0, n_kblk - 1)
            return (bi, blk, 0, sec * nhg + g)
        return pl.BlockSpec((None, NAT_KEY_SUB, w, lw), imap)

    in_specs = ([pl.BlockSpec((None, NAT_Q_ROWS, w, lw), lambda g, bi, r: (bi, r, 0, g))]
                + [kv_spec(j, 1) for j in range(NAT_N_SUB)]
                + [kv_spec(j, 2) for j in range(NAT_N_SUB)]
                + [pl.BlockSpec((None, NAT_Q_ROWS, w, lw), lambda g, bi, r: (bi, r, 0, 3 * nhg + g)),
                   pl.BlockSpec((hb, 2 * NAT_WIN_H - 2, w, 2 * w), lambda g, bi, r: (g, 0, 0, 0))])
    nq = NAT_Q_ROWS * w
    nk = NAT_N_SUB * NAT_KEY_SUB * w
    out = pl.pallas_call(
        functools.partial(_nat_kernel, n_rb=n_rb, heads=hb),
        grid=(nhg, b, n_rb),
        in_specs=in_specs,
        out_specs=pl.BlockSpec((None, NAT_Q_ROWS, w, lw), lambda g, bi, r: (bi, r, 0, g)),
        out_shape=jax.ShapeDtypeStruct((b, rows, w, e), BF16),
        scratch_shapes=[pltpu.VMEM((nq, nk), F32), pltpu.VMEM((nq, nk), BF16)],
        compiler_params=_params("parallel", "parallel", "arbitrary"),
    )(*([x4] * (2 + 2 * NAT_N_SUB)), bias)
    return out.reshape(b * s, e)


def _log_gate(g1, wa2_ref, ba_ref):
    g_hi, g_lo = _split_bf16(g1)
    w_hi = wa2_ref[0]
    w_lo = wa2_ref[1]
    pre = _dot(g_hi, w_hi) + _dot(g_lo, w_hi) + _dot(g_hi, w_lo) + ba_ref[...]
    return (jnp.minimum(pre, 0.0) - jnp.log(1.0 + jnp.exp(-jnp.abs(pre)))) / GLA_GATE_TEMP


def _gla_chunk(q, k, v, la, state_ref, tri, reverse):
    c = GLA_CHUNK
    dk = q.shape[1]
    la_hi, la_lo = _split_bf16(la)
    tri_b = tri.astype(BF16)
    bcum = _dot(tri_b, la_hi) + _dot(tri_b, la_lo)
    mid = c // 2 if reverse else c // 2 - 1
    end = 0 if reverse else c - 1
    b_mid = bcum[mid:mid + 1]
    b_end = bcum[end:end + 1]
    qf = q.astype(F32) * (dk ** -0.5)
    kf = k.astype(F32)
    qa = (qf * jnp.exp(bcum - b_mid)).astype(BF16)
    ka = (kf * jnp.exp(b_mid - bcum)).astype(BF16)
    scores = _dot_nt(qa, ka)
    ii = lax.broadcasted_iota(jnp.int32, (c, c), 0)
    jj = lax.broadcasted_iota(jnp.int32, (c, c), 1)
    keep = (jj > ii) if reverse else (jj <= ii)
    scores = jnp.where(keep, scores, 0.0).astype(BF16)
    o = _dot(scores, v)
    q_in = (qf * jnp.exp(bcum)).astype(BF16)
    o = o + _dot(q_in, state_ref[...].astype(BF16))
    k_out = (kf * jnp.exp(b_end - bcum)).astype(BF16)
    ones = jnp.ones((c, 128), BF16)
    tot = _dot_tn(la_hi, ones) + _dot_tn(la_lo, ones)
    decay = jnp.exp(tot)
    upd = _dot_tn(k_out, v)
    dv = v.shape[1]
    for lb in range(dv // 128):
        sl = slice(lb * 128, (lb + 1) * 128)
        state_ref[:, sl] = decay * state_ref[:, sl] + upd[:, sl]
    return o


def _gla_fwd_kernel(q_ref, k_ref, v_ref, g1_ref, wa2_ref, ba_ref, o_ref, state_ref, *, n_chunks):
    @pl.when(pl.program_id(2) == 0)
    def _():
        state_ref[...] = jnp.zeros_like(state_ref)

    c = GLA_CHUNK
    ii = lax.broadcasted_iota(jnp.int32, (c, c), 0)
    jj = lax.broadcasted_iota(jnp.int32, (c, c), 1)
    tri = (jj <= ii).astype(F32)

    def body(n, carry):
        r = pl.multiple_of(n * c, c)
        rows = pl.ds(r, c)
        la = _log_gate(g1_ref[rows, :GLA_GATE_RANK], wa2_ref, ba_ref)
        o = _gla_chunk(q_ref[rows, :], k_ref[rows, :], v_ref[rows, :], la, state_ref, tri, False)
        o_ref[rows, :] = o.astype(o_ref.dtype)
        return carry
    lax.fori_loop(0, n_chunks, body, 0)


def _gla_bwd_kernel(q_ref, k_ref, v_ref, g1_ref, wa2_ref, ba_ref, of_ref, z_ref, gn_ref, o_ref,
                    state_ref, *, n_chunks):
    @pl.when(pl.program_id(2) == 0)
    def _():
        state_ref[...] = jnp.zeros_like(state_ref)

    c = GLA_CHUNK
    ii = lax.broadcasted_iota(jnp.int32, (c, c), 0)
    jj = lax.broadcasted_iota(jnp.int32, (c, c), 1)
    tri = (jj >= ii).astype(F32)

    def body(n, carry):
        r = pl.multiple_of((n_chunks - 1 - n) * c, c)
        rows = pl.ds(r, c)
        la = _log_gate(g1_ref[rows, GLA_GATE_RANK:], wa2_ref, ba_ref)
        o = _gla_chunk(q_ref[rows, :], k_ref[rows, :], v_ref[rows, :], la, state_ref, tri, True)
        o = o + of_ref[rows, :].astype(F32)
        o = o * lax.rsqrt(jnp.mean(o * o, axis=-1, keepdims=True) + RMS_EPS) * gn_ref[...]
        z = z_ref[rows, :].astype(F32)
        o_ref[rows, :] = (o * _silu(z)).astype(o_ref.dtype)
        return carry
    lax.fori_loop(0, n_chunks, body, 0)


def _gla_mixer(qkvz, g1, wa2_f, ba_f, wa2_b, ba_b, g_norm, b, s):
    hh = GLA_HEADS
    dv = g_norm.shape[0]
    e = hh * dv
    kd = (qkvz.shape[1] - 2 * e) // 2
    dk = kd // hh
    assert kd % dk == 0 and (2 * kd) % dv == 0
    tb = min(512, s)
    assert s % tb == 0 and tb % GLA_CHUNK == 0
    nt = s // tb
    x3 = qkvz.reshape(b, s, qkvz.shape[1])
    g3 = g1.reshape(b, s, 2 * GLA_GATE_RANK)
    nq = kd // dk
    v0 = 2 * kd // dv
    z0 = v0 + hh

    def specs(tmap):
        return [pl.BlockSpec((None, tb, dk), lambda bi, h, t: (bi, tmap(t), h)),
                pl.BlockSpec((None, tb, dk), lambda bi, h, t: (bi, tmap(t), nq + h)),
                pl.BlockSpec((None, tb, dv), lambda bi, h, t: (bi, tmap(t), v0 + h)),
                pl.BlockSpec((None, tb, 2 * GLA_GATE_RANK), lambda bi, h, t: (bi, tmap(t), 0)),
                pl.BlockSpec((2, GLA_GATE_RANK, dk), lambda bi, h, t: (0, 0, h)),
                pl.BlockSpec((1, dk), lambda bi, h, t: (0, h))]

    def gate_w(wa2):
        hi, lo = _split_bf16(wa2)
        return jnp.stack([hi, lo])

    fwd = lambda t: t
    o_f = pl.pallas_call(
        functools.partial(_gla_fwd_kernel, n_chunks=tb // GLA_CHUNK),
        grid=(b, hh, nt),
        in_specs=specs(fwd),
        out_specs=pl.BlockSpec((None, tb, dv), lambda bi, h, t: (bi, t, h)),
        out_shape=jax.ShapeDtypeStruct((b, s, e), BF16),
        scratch_shapes=[pltpu.VMEM((dk, dv), F32)],
        compiler_params=_params("parallel", "parallel", "arbitrary"),
    )(x3, x3, x3, g3, gate_w(wa2_f), ba_f.reshape(1, kd))

    rev = lambda t: nt - 1 - t
    out = pl.pallas_call(
        functools.partial(_gla_bwd_kernel, n_chunks=tb // GLA_CHUNK),
        grid=(b, hh, nt),
        in_specs=specs(rev) + [
            pl.BlockSpec((None, tb, dv), lambda bi, h, t: (bi, rev(t), h)),
            pl.BlockSpec((None, tb, dv), lambda bi, h, t: (bi, rev(t), z0 + h)),
            pl.BlockSpec((1, dv), lambda bi, h, t: (0, 0))],
        out_specs=pl.BlockSpec((None, tb, dv), lambda bi, h, t: (bi, rev(t), h)),
        out_shape=jax.ShapeDtypeStruct((b, s, e), BF16),
        scratch_shapes=[pltpu.VMEM((dk, dv), F32)],
        compiler_params=_params("parallel", "parallel", "arbitrary"),
    )(x3, x3, x3, g3, gate_w(wa2_b), ba_b.reshape(1, kd), o_f, x3, g_norm.reshape(1, dv).astype(F32))
    return out.reshape(b * s, e)


def _fnet_layer(x, g_pre, g_post, w_in, w_out):
    b, s, d = x.shape
    uz = _in_proj(x.reshape(b * s, d), g_pre, w_in.astype(BF16))
    a, n1 = _fnet_mixer(uz, b, s)
    return _out_proj(a, w_out.astype(BF16), g_post, x, transposed_n1=n1)


def _nat_layer(x, g_pre, g_post, w_in, rpb, w_out):
    b, s, d = x.shape
    x2 = x.reshape(b * s, d)
    qkvz = _in_proj(x2, g_pre, w_in.astype(BF16))
    a = _nat_mixer(qkvz, rpb, b, s)
    return _out_proj(a, w_out.astype(BF16), g_post, x2).reshape(b, s, d)


def _gla_layer(x, g_pre, g_post, w_in, wa1_f, wa2_f, ba_f, wa1_b, wa2_b, ba_b, g_norm, w_out):
    b, s, d = x.shape
    x2 = x.reshape(b * s, d)
    qkvz, g1 = _in_proj(x2, g_pre, w_in.astype(BF16), jnp.concatenate([wa1_f, wa1_b], axis=1))
    a = _gla_mixer(qkvz, g1, wa2_f, ba_f, wa2_b, ba_b, g_norm, b, s)
    return _out_proj(a, w_out.astype(BF16), g_post, x2).reshape(b, s, d)


def _trunk(x, norm_pre_g, norm_post_g, fnet_w_in, fnet_w_out, nat_w_in, nat_rpb, nat_w_out,
           gla_w_in, gla_wa1_f, gla_wa2_f, gla_ba_f, gla_wa1_b, gla_wa2_b, gla_ba_b, gla_g_norm,
           gla_w_out):
    depth = norm_pre_g.shape[0]
    for i in range(depth):
        m, j = i % 3, i // 3
        if m == 0:
            x = _fnet_layer(x, norm_pre_g[i], norm_post_g[i], fnet_w_in[j], fnet_w_out[j])
        elif m == 1:
            x = _nat_layer(x, norm_pre_g[i], norm_post_g[i], nat_w_in[j], nat_rpb[j], nat_w_out[j])
        else:
            x = _gla_layer(x, norm_pre_g[i], norm_post_g[i], gla_w_in[j], gla_wa1_f[j], gla_wa2_f[j],
                           gla_ba_f[j], gla_wa1_b[j], gla_wa2_b[j], gla_ba_b[j], gla_g_norm[j],
                           gla_w_out[j])
    return x


def kernel(x_prompt, x_sample, norm_pre_g, norm_post_g, fnet_w_in, fnet_w_out, nat_w_in, nat_rpb,
           nat_w_out, gla_w_in, gla_wa1_f, gla_wa2_f, gla_ba_f, gla_wa1_b, gla_wa2_b, gla_ba_b,
           gla_g_norm, gla_w_out):
    params = (norm_pre_g, norm_post_g, fnet_w_in, fnet_w_out, nat_w_in, nat_rpb, nat_w_out,
              gla_w_in, gla_wa1_f, gla_wa2_f, gla_ba_f, gla_wa1_b, gla_wa2_b, gla_ba_b, gla_g_norm,
              gla_w_out)
    return (_trunk(x_prompt, *params), _trunk(x_sample, *params))
```

```python
import functools
import math

import numpy as np
import jax
import jax.numpy as jnp
from jax import lax
from jax.experimental import pallas as pl
from jax.experimental.pallas import tpu as pltpu

F32 = jnp.float32
BF16 = jnp.bfloat16

RMS_EPS = 1e-6
GRID_W = 64
FNET_GROUP_W = 512
DFT_N2 = 128
NAT_HEAD_DIM = 128
NAT_WIN_H = 8
NAT_WIN_W = 16
NAT_Q_ROWS = 8
NAT_KEY_SUB = 4
NAT_N_SUB = 4
GLA_HEADS = 4
GLA_GATE_RANK = 16
GLA_GATE_TEMP = 16.0
GLA_CHUNK = 64
GLA_SUPER = 256
NEG_MASK = -1e30

V7X_VMEM_LIMIT_BYTES = 58 * 1024 * 1024


def _params(*sem):
    return pltpu.CompilerParams(dimension_semantics=sem, vmem_limit_bytes=V7X_VMEM_LIMIT_BYTES)


def _silu(z):
    return z / (1.0 + jnp.exp(-z))


def _dot(a, b):
    return jnp.dot(a, b, preferred_element_type=F32)


def _dot_nt(a, b):
    return lax.dot_general(a, b, (((1,), (1,)), ((), ())), preferred_element_type=F32)


def _dot_tn(a, b):
    return lax.dot_general(a, b, (((0,), (0,)), ((), ())), preferred_element_type=F32)


def _split_bf16(a):
    hi = a.astype(BF16)
    lo = (a - hi.astype(F32)).astype(BF16)
    return hi, lo


def _in_proj_kernel(*refs, has_aux, row_chunk):
    if has_aux:
        x_ref, g_ref, w_ref, wa_ref, o_ref, aux_ref, h_ref = refs
    else:
        x_ref, g_ref, w_ref, o_ref, h_ref = refs

    @pl.when(pl.program_id(1) == 0)
    def _():
        def body(c, carry):
            r = pl.multiple_of(c * row_chunk, row_chunk)
            x = x_ref[pl.ds(r, row_chunk), :]
            ms = jnp.mean(x * x, axis=-1, keepdims=True)
            hn = x * lax.rsqrt(ms + RMS_EPS) * g_ref[...]
            h_ref[pl.ds(r, row_chunk), :] = hn.astype(BF16)
            if has_aux:
                h_hi, h_lo = _split_bf16(hn)
                w_hi = wa_ref[0]
                w_lo = wa_ref[1]
                aux_ref[pl.ds(r, row_chunk), :] = _dot(h_hi, w_hi) + _dot(h_lo, w_hi) + _dot(h_hi, w_lo)
            return carry
        lax.fori_loop(0, x_ref.shape[0] // row_chunk, body, 0)

    o_ref[...] = _dot(h_ref[...], w_ref[...]).astype(o_ref.dtype)


def _in_proj(x2d, g, w_bf16, w_aux=None):
    t, d = x2d.shape
    n = w_bf16.shape[1]
    tm = min(1024, t)
    tn = min(1024, n)
    assert t % tm == 0 and n % tn == 0 and tm % 128 == 0
    has_aux = w_aux is not None
    in_specs = [
        pl.BlockSpec((tm, d), lambda i, j: (i, 0)),
        pl.BlockSpec((1, d), lambda i, j: (0, 0)),
        pl.BlockSpec((d, tn), lambda i, j: (0, j)),
    ]
    args = [x2d, g.reshape(1, d), w_bf16]
    out_shape = [jax.ShapeDtypeStruct((t, n), BF16)]
    out_specs = [pl.BlockSpec((tm, tn), lambda i, j: (i, j))]
    if has_aux:
        na = w_aux.shape[1]
        hi, lo = _split_bf16(w_aux)
        in_specs.append(pl.BlockSpec((2, d, na), lambda i, j: (0, 0, 0)))
        args.append(jnp.stack([hi, lo]))
        out_shape.append(jax.ShapeDtypeStruct((t, na), F32))
        out_specs.append(pl.BlockSpec((tm, na), lambda i, j: (i, 0)))
    res = pl.pallas_call(
        functools.partial(_in_proj_kernel, has_aux=has_aux, row_chunk=128),
        grid=(t // tm, n // tn),
        in_specs=in_specs,
        out_specs=out_specs,
        out_shape=out_shape,
        scratch_shapes=[pltpu.VMEM((tm, d), BF16)],
        compiler_params=_params("parallel", "arbitrary"),
    )(*args)
    return res if has_aux else res[0]


def _out_proj_kernel(a_ref, w_ref, g_ref, x_ref, o_ref):
    t = _dot(a_ref[...], w_ref[...])
    ms = jnp.mean(t * t, axis=-1, keepdims=True)
    o_ref[...] = x_ref[...] + t * lax.rsqrt(ms + RMS_EPS) * g_ref[...]


def _out_proj(a2d, w_bf16, g, x2d):
    t, e = a2d.shape
    d = w_bf16.shape[1]
    tm = min(512, t)
    assert t % tm == 0
    return pl.pallas_call(
        _out_proj_kernel,
        grid=(t // tm,),
        in_specs=[pl.BlockSpec((tm, e), lambda i: (i, 0)),
                  pl.BlockSpec((e, d), lambda i: (0, 0), pipeline_mode=pl.Buffered(1)),
                  pl.BlockSpec((1, d), lambda i: (0, 0)),
                  pl.BlockSpec((tm, d), lambda i: (i, 0))],
        out_specs=pl.BlockSpec((tm, d), lambda i: (i, 0)),
        out_shape=jax.ShapeDtypeStruct((t, d), F32),
        compiler_params=_params("parallel"),
    )(a2d, w_bf16, g.reshape(1, d), x2d)


def _dft_tables(n1, gb):
    n2 = DFT_N2
    n = n1 * n2
    k1 = np.arange(n1)
    ang_a = 2.0 * np.pi * np.outer(k1, k1) / n1
    eye = np.eye(gb)
    fa_re = np.kron(eye, np.cos(ang_a))
    fa_im = np.kron(eye, -np.sin(ang_a))
    fa = np.concatenate([fa_re, fa_im], axis=0) / math.sqrt(n1)

    kk1 = jnp.arange(n1, dtype=jnp.int32)[:, None, None]
    kk2 = jnp.arange(n2, dtype=jnp.int32)[None, :, None]
    nn2 = jnp.arange(n2, dtype=jnp.int32)[None, None, :]
    m = (nn2 * (kk1 + n1 * kk2)) % n
    ang = m.astype(F32) * (2.0 * math.pi / n)
    c = jnp.cos(ang) / math.sqrt(n2)
    s = jnp.sin(ang) / math.sqrt(n2)
    mtab = jnp.concatenate([jnp.concatenate([c, s], axis=2), jnp.concatenate([-s, c], axis=2)], axis=1)

    cw = FNET_GROUP_W
    ang_c = 2.0 * np.pi * (np.outer(np.arange(cw), np.arange(cw)) % cw) / cw
    cc = np.cos(ang_c) / math.sqrt(cw)
    sc = np.sin(ang_c) / math.sqrt(cw)
    return (jnp.asarray(fa, F32).astype(BF16), mtab.astype(BF16),
            jnp.asarray(cc, F32).astype(BF16), jnp.asarray(sc, F32).astype(BF16))


def _dft_a_kernel(f_ref, u_ref, o_ref):
    o_ref[...] = _dot(f_ref[...], u_ref[...]).astype(o_ref.dtype)


def _dft_c_kernel(a_ref, m_ref, cc_ref, sc_ref, z_ref, o_ref, zr_ref, zi_ref, *, rb, cb):
    n2 = DFT_N2
    for j in range(rb):
        zz = _dot(m_ref[j], a_ref[j])
        zr_ref[j * n2:(j + 1) * n2, :] = zz[:n2].astype(BF16)
        zi_ref[j * n2:(j + 1) * n2, :] = zz[n2:].astype(BF16)
    cw = FNET_GROUP_W
    for gi in range(cb // cw):
        sl = slice(gi * cw, (gi + 1) * cw)
        y = _dot(zr_ref[:, sl], cc_ref[...]) + _dot(zi_ref[:, sl], sc_ref[...])
        z = z_ref[:, sl].astype(F32)
        o_ref[:, sl] = (y * _silu(z)).astype(o_ref.dtype)


def _fnet_mixer(uz, b, s):
    e = uz.shape[1] // 2
    n2 = DFT_N2
    assert s % n2 == 0
    n1 = s // n2
    gb = max(1, min(b, 256 // n1))
    while b % gb:
        gb -= 1
    rg = gb * n1
    ng = b // gb
    fa, mtab, cc, sc = _dft_tables(n1, gb)

    u_t = jnp.transpose(uz.reshape(b, n1, n2, 2 * e)[..., :e], (2, 0, 1, 3)).reshape(n2, ng, rg, e)
    ca = min(4096, e)
    a_nat = pl.pallas_call(
        _dft_a_kernel,
        grid=(n2, ng, e // ca),
        in_specs=[pl.BlockSpec((2 * rg, rg), lambda n, g, c: (0, 0)),
                  pl.BlockSpec((None, None, rg, ca), lambda n, g, c: (n, g, 0, c))],
        out_specs=pl.BlockSpec((None, None, 2 * rg, ca), lambda n, g, c: (n, g, 0, c)),
        out_shape=jax.ShapeDtypeStruct((n2, ng, 2 * rg, e), BF16),
        compiler_params=_params("parallel", "parallel", "parallel"),
    )(fa, u_t)

    r = b * n1
    a3 = jnp.transpose(a_nat.reshape(n2, ng, 2, rg, e), (1, 3, 2, 0, 4)).reshape(r, 2 * n2, e)
    z_t = jnp.transpose(uz.reshape(b, n2, n1, 2 * e)[..., e:], (0, 2, 1, 3)).reshape(r * n2, e)

    rb = min(8, n1)
    assert n1 % rb == 0
    cb = min(1024, e)
    nkb = n1 // rb
    a_t = pl.pallas_call(
        functools.partial(_dft_c_kernel, rb=rb, cb=cb),
        grid=(r // rb, e // cb),
        in_specs=[pl.BlockSpec((rb, 2 * n2, cb), lambda i, c: (i, 0, c)),
                  pl.BlockSpec((rb, 2 * n2, 2 * n2), lambda i, c: (i % nkb, 0, 0)),
                  pl.BlockSpec((FNET_GROUP_W, FNET_GROUP_W), lambda i, c: (0, 0)),
                  pl.BlockSpec((FNET_GROUP_W, FNET_GROUP_W), lambda i, c: (0, 0)),
                  pl.BlockSpec((rb * n2, cb), lambda i, c: (i, c))],
        out_specs=pl.BlockSpec((rb * n2, cb), lambda i, c: (i, c)),
        out_shape=jax.ShapeDtypeStruct((r * n2, e), BF16),
        scratch_shapes=[pltpu.VMEM((rb * n2, cb), BF16), pltpu.VMEM((rb * n2, cb), BF16)],
        compiler_params=_params("parallel", "arbitrary"),
    )(a3, mtab, cc, sc, z_t)
    return jnp.transpose(a_t.reshape(b, n1, n2, e), (0, 2, 1, 3)).reshape(b * s, e)


def _nat_bias_pairs(rpb):
    qc = np.arange(GRID_W)[:, None]
    kc = np.arange(GRID_W)[None, :]
    ws = np.clip(qc - NAT_WIN_W // 2, 0, GRID_W - NAT_WIN_W)
    valid = (kc >= ws) & (kc < ws + NAT_WIN_W)
    rel = np.clip(kc - qc, -(NAT_WIN_W - 1), NAT_WIN_W - 1) + NAT_WIN_W - 1
    colb = jnp.where(jnp.asarray(valid)[None, None], rpb[:, :, jnp.asarray(rel)].astype(F32), NEG_MASK)
    return jnp.concatenate([colb[:, :-1], colb[:, 1:]], axis=-1)


def _nat_kernel(*refs, n_rb, heads):
    q_ref = refs[0]
    k_refs = refs[1:1 + NAT_N_SUB]
    v_refs = refs[1 + NAT_N_SUB:1 + 2 * NAT_N_SUB]
    z_ref, bias_ref, o_ref, s_ref, p_ref = refs[1 + 2 * NAT_N_SUB:]
    hd = NAT_HEAD_DIM
    w = GRID_W
    nq = NAT_Q_ROWS * w
    nks = NAT_KEY_SUB * w
    scale = hd ** -0.5
    lane = lax.broadcasted_iota(jnp.int32, (w, 2 * w), 1)

    def run(window_starts):
        def scores(h, slot):
            off = pl.multiple_of(h * hd, hd)
            q = q_ref[:, :, pl.ds(off, hd)].reshape(nq, hd)
            for j in range(NAT_N_SUB):
                kj = k_refs[j][:, :, pl.ds(off, hd)].reshape(nks, hd)
                s_ref[slot, :, j * nks:(j + 1) * nks] = _dot_nt(q, kj)

        def softmax(h, slot):
            for i in range(NAT_Q_ROWS):
                lo = window_starts[i]
                kp_lo, kp_hi = lo // 2, (lo + NAT_WIN_H - 1) // 2
                rows = slice(i * w, (i + 1) * w)
                tiles = []
                for kp in range(kp_lo, kp_hi + 1):
                    t = (s_ref[slot, rows, kp * 2 * w:(kp + 1) * 2 * w] * scale
                         + bias_ref[h, 2 * kp - i + 3])
                    if 2 * kp < lo:
                        t = jnp.where(lane >= w, t, NEG_MASK)
                    if 2 * kp + 1 >= lo + NAT_WIN_H:
                        t = jnp.where(lane < w, t, NEG_MASK)
                    tiles.append(t)
                m = tiles[0]
                for t in tiles[1:]:
                    m = jnp.maximum(m, t)
                m = jnp.max(m, axis=-1, keepdims=True)
                es = [jnp.exp(t - m) for t in tiles]
                tot = es[0]
                for ee in es[1:]:
                    tot = tot + ee
                inv = 1.0 / jnp.sum(tot, axis=-1, keepdims=True)
                for kp in range(NAT_N_SUB * NAT_KEY_SUB // 2):
                    cols = slice(kp * 2 * w, (kp + 1) * 2 * w)
                    if kp_lo <= kp <= kp_hi:
                        p_ref[slot, rows, cols] = (es[kp - kp_lo] * inv).astype(BF16)
                    else:
                        p_ref[slot, rows, cols] = jnp.zeros((w, 2 * w), BF16)

        def weighted_sum(h, slot):
            off = pl.multiple_of(h * hd, hd)
            acc = None
            for j in range(NAT_N_SUB):
                vj = v_refs[j][:, :, pl.ds(off, hd)].reshape(nks, hd)
                part = _dot(p_ref[slot, :, j * nks:(j + 1) * nks], vj)
                acc = part if acc is None else acc + part
            z = z_ref[:, :, pl.ds(off, hd)].reshape(nq, hd).astype(F32)
            o_ref[:, :, pl.ds(off, hd)] = (acc * _silu(z)).astype(o_ref.dtype).reshape(NAT_Q_ROWS, w, hd)

        def pair_body(t, carry):
            ha, hb = 2 * t, 2 * t + 1
            scores(ha, 0)
            scores(hb, 1)
            softmax(ha, 0)
            softmax(hb, 1)
            weighted_sum(ha, 0)
            weighted_sum(hb, 1)
            return carry
        lax.fori_loop(0, heads // 2, pair_body, 0)

    rb = pl.program_id(2)
    half = NAT_WIN_H // 2
    interior = list(range(NAT_Q_ROWS))
    top = [max(i, half) for i in range(NAT_Q_ROWS)]
    bottom = [min(i, half) for i in range(NAT_Q_ROWS)]

    @pl.when(rb == 0)
    def _():
        run(top)

    @pl.when(rb == n_rb - 1)
    def _():
        run(bottom)

    @pl.when(jnp.logical_and(rb > 0, rb < n_rb - 1))
    def _():
        run(interior)


def _nat_mixer(qkvz, rpb, b, s):
    e = qkvz.shape[1] // 4
    w = GRID_W
    rows = s // w
    assert s % w == 0 and rows % NAT_Q_ROWS == 0 and rows >= 2 * NAT_Q_ROWS
    n_rb = rows // NAT_Q_ROWS
    hb = 8
    lw = hb * NAT_HEAD_DIM
    nhg = e // lw
    x4 = qkvz.reshape(b, rows, w, 4 * e)
    bias = _nat_bias_pairs(rpb)
    n_kblk = rows // NAT_KEY_SUB

    def kv_spec(j, sec):
        def imap(g, bi, r):
            blk = jnp.clip(2 * r - 1 + j, 0, n_kblk - 1)
            return (bi, blk, 0, sec * nhg + g)
        return pl.BlockSpec((None, NAT_KEY_SUB, w, lw), imap)

    in_specs = ([pl.BlockSpec((None, NAT_Q_ROWS, w, lw), lambda g, bi, r: (bi, r, 0, g))]
                + [kv_spec(j, 1) for j in range(NAT_N_SUB)]
                + [kv_spec(j, 2) for j in range(NAT_N_SUB)]
                + [pl.BlockSpec((None, NAT_Q_ROWS, w, lw), lambda g, bi, r: (bi, r, 0, 3 * nhg + g)),
                   pl.BlockSpec((hb, 2 * NAT_WIN_H - 2, w, 2 * w), lambda g, bi, r: (g, 0, 0, 0))])
    nq = NAT_Q_ROWS * w
    nk = NAT_N_SUB * NAT_KEY_SUB * w
    out = pl.pallas_call(
        functools.partial(_nat_kernel, n_rb=n_rb, heads=hb),
        grid=(nhg, b, n_rb),
        in_specs=in_specs,
        out_specs=pl.BlockSpec((None, NAT_Q_ROWS, w, lw), lambda g, bi, r: (bi, r, 0, g)),
        out_shape=jax.ShapeDtypeStruct((b, rows, w, e), BF16),
        scratch_shapes=[pltpu.VMEM((2, nq, nk), F32), pltpu.VMEM((2, nq, nk), BF16)],
        compiler_params=_params("parallel", "parallel", "arbitrary"),
    )(*([x4] * (2 + 2 * NAT_N_SUB)), bias)
    return out.reshape(b * s, e)


def _log_gate(g1, wa2_ref, ba_ref):
    g_hi, g_lo = _split_bf16(g1)
    w_hi = wa2_ref[0]
    w_lo = wa2_ref[1]
    pre = _dot(g_hi, w_hi) + _dot(g_lo, w_hi) + _dot(g_hi, w_lo) + ba_ref[...]
    return (jnp.minimum(pre, 0.0) - jnp.log(1.0 + jnp.exp(-jnp.abs(pre)))) / GLA_GATE_TEMP


def _gla_superchunk(q, k, v, la, state_ref, reverse):
    n, dk = q.shape
    c = GLA_CHUNK
    nb = n // c
    ii = lax.broadcasted_iota(jnp.int32, (n, n), 0)
    jj = lax.broadcasted_iota(jnp.int32, (n, n), 1)
    tri = jnp.where((jj >= ii) if reverse else (jj <= ii), 1.0, 0.0).astype(BF16)
    la_hi, la_lo = _split_bf16(la)
    bc = _dot(tri, la_hi) + _dot(tri, la_lo)
    qf = q.astype(F32) * (dk ** -0.5)
    kf = k.astype(F32)

    def brow(idx):
        return bc[idx:idx + 1]

    ka_parts = []
    mids = []
    for blk in range(nb):
        r0 = blk * c
        b_mid = brow(r0 + c // 2) if reverse else brow(r0 + c // 2 - 1)
        mids.append(b_mid)
        ka_parts.append((kf[r0:r0 + c] * jnp.exp(b_mid - bc[r0:r0 + c])).astype(BF16))
    ka = jnp.concatenate(ka_parts, axis=0)

    row_i = lax.broadcasted_iota(jnp.int32, (c, n), 0)
    col_j = lax.broadcasted_iota(jnp.int32, (c, n), 1)
    s_rows = []
    for blk in range(nb):
        r0 = blk * c
        b_blk = bc[r0:r0 + c]
        q_blk = qf[r0:r0 + c]
        qa = (q_blk * jnp.exp(b_blk - mids[blk])).astype(BF16)
        s_diag = _dot_nt(qa, ka)
        in_blk = jnp.logical_and(col_j >= r0, col_j < r0 + c)
        if reverse:
            keep = jnp.logical_and(in_blk, col_j > row_i + r0)
        else:
            keep = jnp.logical_and(in_blk, col_j <= row_i + r0)
        s_blk = jnp.where(keep, s_diag, 0.0)
        has_other = blk < nb - 1 if reverse else blk > 0
        if has_other:
            b_s = brow(r0 + c) if reverse else brow(r0 - 1)
            qo = (q_blk * jnp.exp(b_blk - b_s)).astype(BF16)
            if reverse:
                ko = (kf[r0 + c:] * jnp.exp(b_s - bc[r0 + c:])).astype(BF16)
                ko = jnp.concatenate([jnp.zeros((r0 + c, dk), BF16), ko], axis=0)
            else:
                ko = (kf[:r0] * jnp.exp(b_s - bc[:r0])).astype(BF16)
                ko = jnp.concatenate([ko, jnp.zeros((n - r0, dk), BF16)], axis=0)
            s_blk = s_blk + _dot_nt(qo, ko)
        s_rows.append(s_blk.astype(BF16))
    scores = jnp.concatenate(s_rows, axis=0)

    b_end = brow(0) if reverse else brow(n - 1)
    q_in = (qf * jnp.exp(bc)).astype(BF16)
    o = _dot(scores, v) + _dot(q_in, state_ref[...].astype(BF16))
    k_out = (kf * jnp.exp(b_end - bc)).astype(BF16)
    ones = jnp.ones((n, 128), BF16)
    decay = jnp.exp(_dot_tn(la_hi, ones) + _dot_tn(la_lo, ones))
    upd = _dot_tn(k_out, v)
    for lb in range(v.shape[1] // 128):
        sl = slice(lb * 128, (lb + 1) * 128)
        state_ref[:, sl] = decay * state_ref[:, sl] + upd[:, sl]
    return o


def _gla_fwd_kernel(q_ref, k_ref, v_ref, g1_ref, wa2_ref, ba_ref, o_ref, state_ref):
    @pl.when(pl.program_id(2) == 0)
    def _():
        state_ref[...] = jnp.zeros_like(state_ref)

    la = _log_gate(g1_ref[:, :GLA_GATE_RANK], wa2_ref, ba_ref)
    n = GLA_SUPER
    for sc in range(q_ref.shape[0] // n):
        rows = slice(sc * n, (sc + 1) * n)
        o = _gla_superchunk(q_ref[rows, :], k_ref[rows, :], v_ref[rows, :], la[rows], state_ref, False)
        o_ref[rows, :] = o.astype(o_ref.dtype)


def _gla_bwd_kernel(q_ref, k_ref, v_ref, g1_ref, wa2_ref, ba_ref, of_ref, z_ref, gn_ref, o_ref,
                    state_ref):
    @pl.when(pl.program_id(2) == 0)
    def _():
        state_ref[...] = jnp.zeros_like(state_ref)

    la = _log_gate(g1_ref[:, GLA_GATE_RANK:], wa2_ref, ba_ref)
    n = GLA_SUPER
    for sc in reversed(range(q_ref.shape[0] // n)):
        rows = slice(sc * n, (sc + 1) * n)
        o = _gla_superchunk(q_ref[rows, :], k_ref[rows, :], v_ref[rows, :], la[rows], state_ref, True)
        o = o + of_ref[rows, :].astype(F32)
        o = o * lax.rsqrt(jnp.mean(o * o, axis=-1, keepdims=True) + RMS_EPS) * gn_ref[...]
        z = z_ref[rows, :].astype(F32)
        o_ref[rows, :] = (o * _silu(z)).astype(o_ref.dtype)


def _gla_mixer(qkvz, g1, wa2_f, ba_f, wa2_b, ba_b, g_norm, b, s):
    hh = GLA_HEADS
    dv = g_norm.shape[0]
    e = hh * dv
    kd = (qkvz.shape[1] - 2 * e) // 2
    dk = kd // hh
    assert kd % dk == 0 and (2 * kd) % dv == 0
    tb = min(512, s)
    assert s % tb == 0 and tb % GLA_SUPER == 0
    nt = s // tb
    x3 = qkvz.reshape(b, s, qkvz.shape[1])
    g3 = g1.reshape(b, s, 2 * GLA_GATE_RANK)
    nq = kd // dk
    v0 = 2 * kd // dv
    z0 = v0 + hh

    def specs(tmap):
        return [pl.BlockSpec((None, tb, dk), lambda bi, h, t: (bi, tmap(t), h)),
                pl.BlockSpec((None, tb, dk), lambda bi, h, t: (bi, tmap(t), nq + h)),
                pl.BlockSpec((None, tb, dv), lambda bi, h, t: (bi, tmap(t), v0 + h)),
                pl.BlockSpec((None, tb, 2 * GLA_GATE_RANK), lambda bi, h, t: (bi, tmap(t), 0)),
                pl.BlockSpec((2, GLA_GATE_RANK, dk), lambda bi, h, t: (0, 0, h)),
                pl.BlockSpec((1, dk), lambda bi, h, t: (0, h))]

    def gate_w(wa2):
        hi, lo = _split_bf16(wa2)
        return jnp.stack([hi, lo])

    fwd = lambda t: t
    o_f = pl.pallas_call(
        _gla_fwd_kernel,
        grid=(b, hh, nt),
        in_specs=specs(fwd),
        out_specs=pl.BlockSpec((None, tb, dv), lambda bi, h, t: (bi, t, h)),
        out_shape=jax.ShapeDtypeStruct((b, s, e), BF16),
        scratch_shapes=[pltpu.VMEM((dk, dv), F32)],
        compiler_params=_params("parallel", "parallel", "arbitrary"),
    )(x3, x3, x3, g3, gate_w(wa2_f), ba_f.reshape(1, kd))

    rev = lambda t: nt - 1 - t
    out = pl.pallas_call(
        _gla_bwd_kernel,
        grid=(b, hh, nt),
        in_specs=specs(rev) + [
            pl.BlockSpec((None, tb, dv), lambda bi, h, t: (bi, rev(t), h)),
            pl.BlockSpec((None, tb, dv), lambda bi, h, t: (bi, rev(t), z0 + h)),
            pl.BlockSpec((1, dv), lambda bi, h, t: (0, 0))],
        out_specs=pl.BlockSpec((None, tb, dv), lambda bi, h, t: (bi, rev(t), h)),
        out_shape=jax.ShapeDtypeStruct((b, s, e), BF16),
        scratch_shapes=[pltpu.VMEM((dk, dv), F32)],
        compiler_params=_params("parallel", "parallel", "arbitrary"),
    )(x3, x3, x3, g3, gate_w(wa2_b), ba_b.reshape(1, kd), o_f, x3, g_norm.reshape(1, dv).astype(F32))
    return out.reshape(b * s, e)


def _fnet_layer(x, g_pre, g_post, w_in, w_out):
    b, s, d = x.shape
    x2 = x.reshape(b * s, d)
    uz = _in_proj(x2, g_pre, w_in.astype(BF16))
    a = _fnet_mixer(uz, b, s)
    return _out_proj(a, w_out.astype(BF16), g_post, x2).reshape(b, s, d)


def _nat_layer(x, g_pre, g_post, w_in, rpb, w_out):
    b, s, d = x.shape
    x2 = x.reshape(b * s, d)
    qkvz = _in_proj(x2, g_pre, w_in.astype(BF16))
    a = _nat_mixer(qkvz, rpb, b, s)
    return _out_proj(a, w_out.astype(BF16), g_post, x2).reshape(b, s, d)


def _gla_layer(x, g_pre, g_post, w_in, wa1_f, wa2_f, ba_f, wa1_b, wa2_b, ba_b, g_norm, w_out):
    b, s, d = x.shape
    x2 = x.reshape(b * s, d)
    qkvz, g1 = _in_proj(x2, g_pre, w_in.astype(BF16), jnp.concatenate([wa1_f, wa1_b], axis=1))
    a = _gla_mixer(qkvz, g1, wa2_f, ba_f, wa2_b, ba_b, g_norm, b, s)
    return _out_proj(a, w_out.astype(BF16), g_post, x2).reshape(b, s, d)


def _trunk(x, norm_pre_g, norm_post_g, fnet_w_in, fnet_w_out, nat_w_in, nat_rpb, nat_w_out,
           gla_w_in, gla_wa1_f, gla_wa2_f, gla_ba_f, gla_wa1_b, gla_wa2_b, gla_ba_b, gla_g_norm,
           gla_w_out):
    depth = norm_pre_g.shape[0]
    for i in range(depth):
        m, j = i % 3, i // 3
        if m == 0:
            x = _fnet_layer(x, norm_pre_g[i], norm_post_g[i], fnet_w_in[j], fnet_w_out[j])
        elif m == 1:
            x = _nat_layer(x, norm_pre_g[i], norm_post_g[i], nat_w_in[j], nat_rpb[j], nat_w_out[j])
        else:
            x = _gla_layer(x, norm_pre_g[i], norm_post_g[i], gla_w_in[j], gla_wa1_f[j], gla_wa2_f[j],
                           gla_ba_f[j], gla_wa1_b[j], gla_wa2_b[j], gla_ba_b[j], gla_g_norm[j],
                           gla_w_out[j])
    return x


def kernel(x_prompt, x_sample, norm_pre_g, norm_post_g, fnet_w_in, fnet_w_out, nat_w_in, nat_rpb,
           nat_w_out, gla_w_in, gla_wa1_f, gla_wa2_f, gla_ba_f, gla_wa1_b, gla_wa2_b, gla_ba_b,
           gla_g_norm, gla_w_out):
    params = (norm_pre_g, norm_post_g, fnet_w_in, fnet_w_out, nat_w_in, nat_rpb, nat_w_out,
              gla_w_in, gla_wa1_f, gla_wa2_f, gla_ba_f, gla_wa1_b, gla_wa2_b, gla_ba_b, gla_g_norm,
              gla_w_out)
    return (_trunk(x_prompt, *params), _trunk(x_sample, *params))
```

```python
import functools
import math

import numpy as np
import jax
import jax.numpy as jnp
from jax import lax
from jax.experimental import pallas as pl
from jax.experimental.pallas import tpu as pltpu

F32 = jnp.float32
BF16 = jnp.bfloat16

RMS_EPS = 1e-6
GRID_W = 64
FNET_GROUP_W = 512
DFT_N2 = 128
NAT_HEAD_DIM = 128
NAT_WIN_H = 8
NAT_WIN_W = 16
NAT_Q_ROWS = 8
NAT_KEY_SUB = 4
NAT_N_SUB = 4
NAT_HEAD_GROUP = 4
GLA_HEADS = 4
GLA_GATE_RANK = 16
GLA_GATE_TEMP = 16.0
GLA_CHUNK = 64
GLA_SUPER = 256
GLA_HEADS_PER_STEP = 2
NEG_MASK = -1e30
LOG2_E = 1.4426950408889634

V7X_VMEM_LIMIT_BYTES = 58 * 1024 * 1024


def _params(*sem):
    return pltpu.CompilerParams(dimension_semantics=sem, vmem_limit_bytes=V7X_VMEM_LIMIT_BYTES)


def _silu(z):
    return z / (1.0 + jnp.exp(-z))


def _dot(a, b):
    return jnp.dot(a, b, preferred_element_type=F32)


def _dot_nt(a, b):
    return lax.dot_general(a, b, (((1,), (1,)), ((), ())), preferred_element_type=F32)


def _dot_tn(a, b):
    return lax.dot_general(a, b, (((0,), (0,)), ((), ())), preferred_element_type=F32)


def _split_bf16(a):
    hi = a.astype(BF16)
    lo = (a - hi.astype(F32)).astype(BF16)
    return hi, lo


def _in_proj_kernel(*refs, has_aux, row_chunk):
    if has_aux:
        x_ref, g_ref, w_ref, wa_ref, o_ref, aux_ref, h_ref = refs
    else:
        x_ref, g_ref, w_ref, o_ref, h_ref = refs

    @pl.when(pl.program_id(1) == 0)
    def _():
        def body(c, carry):
            r = pl.multiple_of(c * row_chunk, row_chunk)
            x = x_ref[pl.ds(r, row_chunk), :]
            ms = jnp.mean(x * x, axis=-1, keepdims=True)
            hn = x * lax.rsqrt(ms + RMS_EPS) * g_ref[...]
            h_ref[pl.ds(r, row_chunk), :] = hn.astype(BF16)
            if has_aux:
                h_hi, h_lo = _split_bf16(hn)
                w_hi = wa_ref[0]
                w_lo = wa_ref[1]
                aux_ref[pl.ds(r, row_chunk), :] = _dot(h_hi, w_hi) + _dot(h_lo, w_hi) + _dot(h_hi, w_lo)
            return carry
        lax.fori_loop(0, x_ref.shape[0] // row_chunk, body, 0)

    o_ref[...] = _dot(h_ref[...], w_ref[...]).astype(o_ref.dtype)


def _in_proj(x2d, g, w_bf16, w_aux=None):
    t, d = x2d.shape
    n = w_bf16.shape[1]
    tm = min(1024, t)
    tn = min(2048, n)
    assert t % tm == 0 and n % tn == 0 and tm % 128 == 0
    has_aux = w_aux is not None
    in_specs = [
        pl.BlockSpec((tm, d), lambda i, j: (i, 0)),
        pl.BlockSpec((1, d), lambda i, j: (0, 0)),
        pl.BlockSpec((d, tn), lambda i, j: (0, j)),
    ]
    args = [x2d, g.reshape(1, d), w_bf16]
    out_shape = [jax.ShapeDtypeStruct((t, n), BF16)]
    out_specs = [pl.BlockSpec((tm, tn), lambda i, j: (i, j))]
    if has_aux:
        na = w_aux.shape[1]
        hi, lo = _split_bf16(w_aux)
        in_specs.append(pl.BlockSpec((2, d, na), lambda i, j: (0, 0, 0)))
        args.append(jnp.stack([hi, lo]))
        out_shape.append(jax.ShapeDtypeStruct((t, na), F32))
        out_specs.append(pl.BlockSpec((tm, na), lambda i, j: (i, 0)))
    res = pl.pallas_call(
        functools.partial(_in_proj_kernel, has_aux=has_aux, row_chunk=128),
        grid=(t // tm, n // tn),
        in_specs=in_specs,
        out_specs=out_specs,
        out_shape=out_shape,
        scratch_shapes=[pltpu.VMEM((tm, d), BF16)],
        compiler_params=_params("parallel", "arbitrary"),
    )(*args)
    return res if has_aux else res[0]


def _out_proj_kernel(a_ref, w_ref, g_ref, x_ref, o_ref):
    t = _dot(a_ref[...], w_ref[...])
    ms = jnp.mean(t * t, axis=-1, keepdims=True)
    o_ref[...] = x_ref[...] + t * lax.rsqrt(ms + RMS_EPS) * g_ref[...]


def _out_proj(a2d, w_bf16, g, x2d):
    t, e = a2d.shape
    d = w_bf16.shape[1]
    tm = min(512, t)
    assert t % tm == 0
    return pl.pallas_call(
        _out_proj_kernel,
        grid=(t // tm,),
        in_specs=[pl.BlockSpec((tm, e), lambda i: (i, 0)),
                  pl.BlockSpec((e, d), lambda i: (0, 0), pipeline_mode=pl.Buffered(1)),
                  pl.BlockSpec((1, d), lambda i: (0, 0)),
                  pl.BlockSpec((tm, d), lambda i: (i, 0))],
        out_specs=pl.BlockSpec((tm, d), lambda i: (i, 0)),
        out_shape=jax.ShapeDtypeStruct((t, d), F32),
        compiler_params=_params("parallel"),
    )(a2d, w_bf16, g.reshape(1, d), x2d)


def _dft_tables(n1, gb):
    n2 = DFT_N2
    n = n1 * n2
    k1 = np.arange(n1)
    ang_a = 2.0 * np.pi * np.outer(k1, k1) / n1
    eye = np.eye(gb)
    fa_re = np.kron(eye, np.cos(ang_a))
    fa_im = np.kron(eye, -np.sin(ang_a))
    fa = np.concatenate([fa_re, fa_im], axis=0) / math.sqrt(n1)

    kk1 = jnp.arange(n1, dtype=jnp.int32)[:, None, None]
    kk2 = jnp.arange(n2, dtype=jnp.int32)[None, :, None]
    nn2 = jnp.arange(n2, dtype=jnp.int32)[None, None, :]
    m = (nn2 * (kk1 + n1 * kk2)) % n
    ang = m.astype(F32) * (2.0 * math.pi / n)
    c = jnp.cos(ang) / math.sqrt(n2)
    s = jnp.sin(ang) / math.sqrt(n2)
    mtab = jnp.concatenate([jnp.concatenate([c, s], axis=2), jnp.concatenate([-s, c], axis=2)], axis=1)

    cw = FNET_GROUP_W
    ang_c = 2.0 * np.pi * (np.outer(np.arange(cw), np.arange(cw)) % cw) / cw
    cc = np.cos(ang_c) / math.sqrt(cw)
    sc = np.sin(ang_c) / math.sqrt(cw)
    return (jnp.asarray(fa, F32).astype(BF16), mtab.astype(BF16),
            jnp.asarray(cc, F32).astype(BF16), jnp.asarray(sc, F32).astype(BF16))


def _dft_a_kernel(f_ref, u_ref, o_ref):
    o_ref[...] = _dot(f_ref[...], u_ref[...]).astype(o_ref.dtype)


def _dft_c_kernel(a_ref, m_ref, cc_ref, sc_ref, z_ref, o_ref, zr_ref, zi_ref, *, rb, cb):
    n2 = DFT_N2
    for j in range(rb):
        zz = _dot(m_ref[j], a_ref[j])
        zr_ref[j * n2:(j + 1) * n2, :] = zz[:n2].astype(BF16)
        zi_ref[j * n2:(j + 1) * n2, :] = zz[n2:].astype(BF16)
    cw = FNET_GROUP_W
    for gi in range(cb // cw):
        sl = slice(gi * cw, (gi + 1) * cw)
        y = _dot(zr_ref[:, sl], cc_ref[...]) + _dot(zi_ref[:, sl], sc_ref[...])
        z = z_ref[:, sl].astype(F32)
        o_ref[:, sl] = (y * _silu(z)).astype(o_ref.dtype)


def _fnet_mixer(uz, b, s):
    e = uz.shape[1] // 2
    n2 = DFT_N2
    assert s % n2 == 0
    n1 = s // n2
    gb = max(1, min(b, 256 // n1))
    while b % gb:
        gb -= 1
    rg = gb * n1
    ng = b // gb
    fa, mtab, cc, sc = _dft_tables(n1, gb)

    u_t = jnp.transpose(uz.reshape(b, n1, n2, 2 * e)[..., :e], (2, 0, 1, 3)).reshape(n2, ng, rg, e)
    ca = min(4096, e)
    a_nat = pl.pallas_call(
        _dft_a_kernel,
        grid=(n2, ng, e // ca),
        in_specs=[pl.BlockSpec((2 * rg, rg), lambda n, g, c: (0, 0)),
                  pl.BlockSpec((None, None, rg, ca), lambda n, g, c: (n, g, 0, c))],
        out_specs=pl.BlockSpec((None, None, 2 * rg, ca), lambda n, g, c: (n, g, 0, c)),
        out_shape=jax.ShapeDtypeStruct((n2, ng, 2 * rg, e), BF16),
        compiler_params=_params("parallel", "parallel", "parallel"),
    )(fa, u_t)

    r = b * n1
    a3 = jnp.transpose(a_nat.reshape(n2, ng, 2, rg, e), (1, 3, 2, 0, 4)).reshape(r, 2 * n2, e)
    z_t = jnp.transpose(uz.reshape(b, n2, n1, 2 * e)[..., e:], (0, 2, 1, 3)).reshape(r * n2, e)

    rb = min(8, n1)
    assert n1 % rb == 0
    cb = min(1024, e)
    nkb = n1 // rb
    a_t = pl.pallas_call(
        functools.partial(_dft_c_kernel, rb=rb, cb=cb),
        grid=(r // rb, e // cb),
        in_specs=[pl.BlockSpec((rb, 2 * n2, cb), lambda i, c: (i, 0, c)),
                  pl.BlockSpec((rb, 2 * n2, 2 * n2), lambda i, c: (i % nkb, 0, 0)),
                  pl.BlockSpec((FNET_GROUP_W, FNET_GROUP_W), lambda i, c: (0, 0)),
                  pl.BlockSpec((FNET_GROUP_W, FNET_GROUP_W), lambda i, c: (0, 0)),
                  pl.BlockSpec((rb * n2, cb), lambda i, c: (i, c))],
        out_specs=pl.BlockSpec((rb * n2, cb), lambda i, c: (i, c)),
        out_shape=jax.ShapeDtypeStruct((r * n2, e), BF16),
        scratch_shapes=[pltpu.VMEM((rb * n2, cb), BF16), pltpu.VMEM((rb * n2, cb), BF16)],
        compiler_params=_params("parallel", "arbitrary"),
    )(a3, mtab, cc, sc, z_t)
    return jnp.transpose(a_t.reshape(b, n1, n2, e), (0, 2, 1, 3)).reshape(b * s, e)


def _nat_bias_pairs(rpb):
    qc = np.arange(GRID_W)[:, None]
    kc = np.arange(GRID_W)[None, :]
    ws = np.clip(qc - NAT_WIN_W // 2, 0, GRID_W - NAT_WIN_W)
    valid = (kc >= ws) & (kc < ws + NAT_WIN_W)
    rel = np.clip(kc - qc, -(NAT_WIN_W - 1), NAT_WIN_W - 1) + NAT_WIN_W - 1
    colb = jnp.where(jnp.asarray(valid)[None, None], rpb[:, :, jnp.asarray(rel)].astype(F32), NEG_MASK)
    return jnp.concatenate([colb[:, :-1], colb[:, 1:]], axis=-1) * LOG2_E


def _nat_kernel(*refs, n_rb, heads):
    q_ref = refs[0]
    k_refs = refs[1:1 + NAT_N_SUB]
    v_refs = refs[1 + NAT_N_SUB:1 + 2 * NAT_N_SUB]
    z_ref, bias_ref, o_ref, s_ref, p_ref = refs[1 + 2 * NAT_N_SUB:]
    hd = NAT_HEAD_DIM
    w = GRID_W
    nq = NAT_Q_ROWS * w
    nks = NAT_KEY_SUB * w
    lane = lax.broadcasted_iota(jnp.int32, (w, 2 * w), 1)

    rows_half = NAT_Q_ROWS // 2
    hq = rows_half * w
    subs_half = NAT_N_SUB - 1
    pairs_sub = NAT_KEY_SUB // 2

    def run(window_starts):
        def scores(h, slot):
            off = pl.multiple_of(h * hd, hd)
            ks = [k_refs[j][:, :, pl.ds(off, hd)].reshape(nks, hd) for j in range(NAT_N_SUB)]
            for a in range(2):
                q = q_ref[a * rows_half:(a + 1) * rows_half, :, pl.ds(off, hd)].reshape(hq, hd)
                for j in range(a, a + subs_half):
                    s_ref[slot, a * hq:(a + 1) * hq, j * nks:(j + 1) * nks] = _dot_nt(q, ks[j])

        def softmax(h, slot):
            for i in range(NAT_Q_ROWS):
                lo = window_starts[i]
                kp_lo, kp_hi = lo // 2, (lo + NAT_WIN_H - 1) // 2
                a = i // rows_half
                assert a * pairs_sub <= kp_lo and kp_hi < (a + subs_half) * pairs_sub
                rows = slice(i * w, (i + 1) * w)
                tiles = []
                for kp in range(kp_lo, kp_hi + 1):
                    t = s_ref[slot, rows, kp * 2 * w:(kp + 1) * 2 * w] + bias_ref[h, 2 * kp - i + 3]
                    if 2 * kp < lo:
                        t = jnp.where(lane >= w, t, NEG_MASK)
                    if 2 * kp + 1 >= lo + NAT_WIN_H:
                        t = jnp.where(lane < w, t, NEG_MASK)
                    tiles.append(t)
                m = tiles[0]
                for t in tiles[1:]:
                    m = jnp.maximum(m, t)
                m = jnp.max(m, axis=-1, keepdims=True)
                es = [jnp.exp2(t - m) for t in tiles]
                tot = es[0]
                for ee in es[1:]:
                    tot = tot + ee
                inv = 1.0 / jnp.sum(tot, axis=-1, keepdims=True)
                for kp in range(a * pairs_sub, (a + subs_half) * pairs_sub):
                    cols = slice(kp * 2 * w, (kp + 1) * 2 * w)
                    if kp_lo <= kp <= kp_hi:
                        p_ref[slot, rows, cols] = (es[kp - kp_lo] * inv).astype(BF16)
                    else:
                        p_ref[slot, rows, cols] = jnp.zeros((w, 2 * w), BF16)

        def weighted_sum(h, slot):
            off = pl.multiple_of(h * hd, hd)
            vs = [v_refs[j][:, :, pl.ds(off, hd)].reshape(nks, hd) for j in range(NAT_N_SUB)]
            for a in range(2):
                acc = None
                for j in range(a, a + subs_half):
                    part = _dot(p_ref[slot, a * hq:(a + 1) * hq, j * nks:(j + 1) * nks], vs[j])
                    acc = part if acc is None else acc + part
                qrows = slice(a * rows_half, (a + 1) * rows_half)
                z = z_ref[qrows, :, pl.ds(off, hd)].reshape(hq, hd).astype(F32)
                o_ref[qrows, :, pl.ds(off, hd)] = (
                    (acc * _silu(z)).astype(o_ref.dtype).reshape(rows_half, w, hd))

        def group_body(t, carry):
            hs = [NAT_HEAD_GROUP * t + g for g in range(NAT_HEAD_GROUP)]
            for g, h in enumerate(hs):
                scores(h, g)
            for g, h in enumerate(hs):
                softmax(h, g)
            for g, h in enumerate(hs):
                weighted_sum(h, g)
            return carry
        lax.fori_loop(0, heads // NAT_HEAD_GROUP, group_body, 0)

    rb = pl.program_id(2)
    half = NAT_WIN_H // 2
    interior = list(range(NAT_Q_ROWS))
    top = [max(i, half) for i in range(NAT_Q_ROWS)]
    bottom = [min(i, half) for i in range(NAT_Q_ROWS)]

    @pl.when(rb == 0)
    def _():
        run(top)

    @pl.when(rb == n_rb - 1)
    def _():
        run(bottom)

    @pl.when(jnp.logical_and(rb > 0, rb < n_rb - 1))
    def _():
        run(interior)


def _nat_mixer(qkvz, rpb, b, s):
    e = qkvz.shape[1] // 4
    w = GRID_W
    rows = s // w
    assert s % w == 0 and rows % NAT_Q_ROWS == 0 and rows >= 2 * NAT_Q_ROWS
    n_rb = rows // NAT_Q_ROWS
    hb = 8
    lw = hb * NAT_HEAD_DIM
    nhg = e // lw
    x4 = qkvz.reshape(b, rows, w, 4 * e)
    bias = _nat_bias_pairs(rpb)
    n_kblk = rows // NAT_KEY_SUB

    def kv_spec(j, sec):
        def imap(g, bi, r):
            blk = jnp.clip(2 * r - 1 + j, 0, n_kblk - 1)
            return (bi, blk, 0, sec * nhg + g)
        return pl.BlockSpec((None, NAT_KEY_SUB, w, lw), imap)

    in_specs = ([pl.BlockSpec((None, NAT_Q_ROWS, w, lw), lambda g, bi, r: (bi, r, 0, g))]
                + [kv_spec(j, 1) for j in range(NAT_N_SUB)]
                + [kv_spec(j, 2) for j in range(NAT_N_SUB)]
                + [pl.BlockSpec((None, NAT_Q_ROWS, w, lw), lambda g, bi, r: (bi, r, 0, 3 * nhg + g)),
                   pl.BlockSpec((hb, 2 * NAT_WIN_H - 2, w, 2 * w), lambda g, bi, r: (g, 0, 0, 0))])
    nq = NAT_Q_ROWS * w
    nk = NAT_N_SUB * NAT_KEY_SUB * w
    out = pl.pallas_call(
        functools.partial(_nat_kernel, n_rb=n_rb, heads=hb),
        grid=(nhg, b, n_rb),
        in_specs=in_specs,
        out_specs=pl.BlockSpec((None, NAT_Q_ROWS, w, lw), lambda g, bi, r: (bi, r, 0, g)),
        out_shape=jax.ShapeDtypeStruct((b, rows, w, e), BF16),
        scratch_shapes=[pltpu.VMEM((NAT_HEAD_GROUP, nq, nk), F32),
                        pltpu.VMEM((NAT_HEAD_GROUP, nq, nk), BF16)],
        compiler_params=_params("parallel", "parallel", "arbitrary"),
    )(*([x4] * (2 + 2 * NAT_N_SUB)), bias)
    return out.reshape(b * s, e)


def _log_gate(g1, wa2_ref, ba_ref):
    g_hi, g_lo = _split_bf16(g1)
    w_hi = wa2_ref[0]
    w_lo = wa2_ref[1]
    pre = _dot(g_hi, w_hi) + _dot(g_lo, w_hi) + _dot(g_hi, w_lo) + ba_ref[...]
    return (jnp.minimum(pre, 0.0) - jnp.log(1.0 + jnp.exp(-jnp.abs(pre)))) / GLA_GATE_TEMP


def _gla_superchunk(q, k, v, la, state_ref, reverse):
    n, dk = q.shape
    c = GLA_CHUNK
    nb = n // c
    ii = lax.broadcasted_iota(jnp.int32, (n, n), 0)
    jj = lax.broadcasted_iota(jnp.int32, (n, n), 1)
    tri = jnp.where((jj >= ii) if reverse else (jj <= ii), 1.0, 0.0).astype(BF16)
    la_hi, la_lo = _split_bf16(la)
    bc = _dot(tri, la_hi) + _dot(tri, la_lo)
    qf = q.astype(F32) * (dk ** -0.5)
    kf = k.astype(F32)

    def brow(idx):
        return bc[idx:idx + 1]

    ka_parts = []
    mids = []
    for blk in range(nb):
        r0 = blk * c
        b_mid = brow(r0 + c // 2) if reverse else brow(r0 + c // 2 - 1)
        mids.append(b_mid)
        ka_parts.append((kf[r0:r0 + c] * jnp.exp(b_mid - bc[r0:r0 + c])).astype(BF16))
    ka = jnp.concatenate(ka_parts, axis=0)

    row_i = lax.broadcasted_iota(jnp.int32, (c, n), 0)
    col_j = lax.broadcasted_iota(jnp.int32, (c, n), 1)
    s_rows = []
    for blk in range(nb):
        r0 = blk * c
        b_blk = bc[r0:r0 + c]
        q_blk = qf[r0:r0 + c]
        qa = (q_blk * jnp.exp(b_blk - mids[blk])).astype(BF16)
        s_diag = _dot_nt(qa, ka)
        in_blk = jnp.logical_and(col_j >= r0, col_j < r0 + c)
        if reverse:
            keep = jnp.logical_and(in_blk, col_j > row_i + r0)
        else:
            keep = jnp.logical_and(in_blk, col_j <= row_i + r0)
        s_blk = jnp.where(keep, s_diag, 0.0)
        has_other = blk < nb - 1 if reverse else blk > 0
        if has_other:
            b_s = brow(r0 + c) if reverse else brow(r0 - 1)
            qo = (q_blk * jnp.exp(b_blk - b_s)).astype(BF16)
            if reverse:
                ko = (kf[r0 + c:] * jnp.exp(b_s - bc[r0 + c:])).astype(BF16)
                ko = jnp.concatenate([jnp.zeros((r0 + c, dk), BF16), ko], axis=0)
            else:
                ko = (kf[:r0] * jnp.exp(b_s - bc[:r0])).astype(BF16)
                ko = jnp.concatenate([ko, jnp.zeros((n - r0, dk), BF16)], axis=0)
            s_blk = s_blk + _dot_nt(qo, ko)
        s_rows.append(s_blk.astype(BF16))
    scores = jnp.concatenate(s_rows, axis=0)

    b_end = brow(0) if reverse else brow(n - 1)
    q_in = (qf * jnp.exp(bc)).astype(BF16)
    o = _dot(scores, v) + _dot(q_in, state_ref[...].astype(BF16))
    k_out = (kf * jnp.exp(b_end - bc)).astype(BF16)
    ones = jnp.ones((n, 128), BF16)
    decay = jnp.exp(_dot_tn(la_hi, ones) + _dot_tn(la_lo, ones))
    upd = _dot_tn(k_out, v)
    for lb in range(v.shape[1] // 128):
        sl = slice(lb * 128, (lb + 1) * 128)
        state_ref[:, sl] = decay * state_ref[:, sl] + upd[:, sl]
    return o


def _gla_fwd_kernel(q_ref, k_ref, v_ref, g1_ref, wa2_ref, ba_ref, o_ref, state_ref):
    @pl.when(pl.program_id(2) == 0)
    def _():
        state_ref[...] = jnp.zeros_like(state_ref)

    n_heads, dk, dv = state_ref.shape
    la = _log_gate(g1_ref[:, :GLA_GATE_RANK], wa2_ref, ba_ref)
    n = GLA_SUPER
    for sc in range(q_ref.shape[0] // n):
        rows = slice(sc * n, (sc + 1) * n)
        for g in range(n_heads):
            kl = slice(g * dk, (g + 1) * dk)
            vl = slice(g * dv, (g + 1) * dv)
            o = _gla_superchunk(q_ref[rows, kl], k_ref[rows, kl], v_ref[rows, vl], la[rows, kl],
                                state_ref.at[g], False)
            o_ref[rows, vl] = o.astype(o_ref.dtype)


def _gla_bwd_kernel(q_ref, k_ref, v_ref, g1_ref, wa2_ref, ba_ref, of_ref, z_ref, gn_ref, o_ref,
                    state_ref):
    @pl.when(pl.program_id(2) == 0)
    def _():
        state_ref[...] = jnp.zeros_like(state_ref)

    n_heads, dk, dv = state_ref.shape
    la = _log_gate(g1_ref[:, GLA_GATE_RANK:], wa2_ref, ba_ref)
    n = GLA_SUPER
    for sc in reversed(range(q_ref.shape[0] // n)):
        rows = slice(sc * n, (sc + 1) * n)
        for g in range(n_heads):
            kl = slice(g * dk, (g + 1) * dk)
            vl = slice(g * dv, (g + 1) * dv)
            o = _gla_superchunk(q_ref[rows, kl], k_ref[rows, kl], v_ref[rows, vl], la[rows, kl],
                                state_ref.at[g], True)
            o = o + of_ref[rows, vl].astype(F32)
            o = o * lax.rsqrt(jnp.mean(o * o, axis=-1, keepdims=True) + RMS_EPS) * gn_ref[...]
            z = z_ref[rows, vl].astype(F32)
            o_ref[rows, vl] = (o * _silu(z)).astype(o_ref.dtype)


def _gla_mixer(qkvz, g1, wa2_f, ba_f, wa2_b, ba_b, g_norm, b, s):
    hh = GLA_HEADS
    dv = g_norm.shape[0]
    e = hh * dv
    kd = (qkvz.shape[1] - 2 * e) // 2
    dk = kd // hh
    hp = GLA_HEADS_PER_STEP
    wk, wv = hp * dk, hp * dv
    assert hh % hp == 0 and kd % wk == 0 and (2 * kd) % wv == 0
    tb = min(512, s)
    assert s % tb == 0 and tb % GLA_SUPER == 0
    nt = s // tb
    x3 = qkvz.reshape(b, s, qkvz.shape[1])
    g3 = g1.reshape(b, s, 2 * GLA_GATE_RANK)
    k0 = kd // wk
    v0 = 2 * kd // wv
    z0 = v0 + hh // hp

    def specs(tmap):
        return [pl.BlockSpec((None, tb, wk), lambda bi, h, t: (bi, tmap(t), h)),
                pl.BlockSpec((None, tb, wk), lambda bi, h, t: (bi, tmap(t), k0 + h)),
                pl.BlockSpec((None, tb, wv), lambda bi, h, t: (bi, tmap(t), v0 + h)),
                pl.BlockSpec((None, tb, 2 * GLA_GATE_RANK), lambda bi, h, t: (bi, tmap(t), 0)),
                pl.BlockSpec((2, GLA_GATE_RANK, wk), lambda bi, h, t: (0, 0, h)),
                pl.BlockSpec((1, wk), lambda bi, h, t: (0, h))]

    def gate_w(wa2):
        hi, lo = _split_bf16(wa2)
        return jnp.stack([hi, lo])

    fwd = lambda t: t
    o_f = pl.pallas_call(
        _gla_fwd_kernel,
        grid=(b, hh // hp, nt),
        in_specs=specs(fwd),
        out_specs=pl.BlockSpec((None, tb, wv), lambda bi, h, t: (bi, t, h)),
        out_shape=jax.ShapeDtypeStruct((b, s, e), BF16),
        scratch_shapes=[pltpu.VMEM((hp, dk, dv), F32)],
        compiler_params=_params("parallel", "parallel", "arbitrary"),
    )(x3, x3, x3, g3, gate_w(wa2_f), ba_f.reshape(1, kd))

    rev = lambda t: nt - 1 - t
    out = pl.pallas_call(
        _gla_bwd_kernel,
        grid=(b, hh // hp, nt),
        in_specs=specs(rev) + [
            pl.BlockSpec((None, tb, wv), lambda bi, h, t: (bi, rev(t), h)),
            pl.BlockSpec((None, tb, wv), lambda bi, h, t: (bi, rev(t), z0 + h)),
            pl.BlockSpec((1, dv), lambda bi, h, t: (0, 0))],
        out_specs=pl.BlockSpec((None, tb, wv), lambda bi, h, t: (bi, rev(t), h)),
        out_shape=jax.ShapeDtypeStruct((b, s, e), BF16),
        scratch_shapes=[pltpu.VMEM((hp, dk, dv), F32)],
        compiler_params=_params("parallel", "parallel", "arbitrary"),
    )(x3, x3, x3, g3, gate_w(wa2_b), ba_b.reshape(1, kd), o_f, x3, g_norm.reshape(1, dv).astype(F32))
    return out.reshape(b * s, e)


def _fnet_layer(x, g_pre, g_post, w_in, w_out):
    b, s, d = x.shape
    x2 = x.reshape(b * s, d)
    uz = _in_proj(x2, g_pre, w_in.astype(BF16))
    a = _fnet_mixer(uz, b, s)
    return _out_proj(a, w_out.astype(BF16), g_post, x2).reshape(b, s, d)


def _nat_layer(x, g_pre, g_post, w_in, rpb, w_out):
    b, s, d = x.shape
    x2 = x.reshape(b * s, d)
    e = w_in.shape[1] // 4
    q_cols = (jnp.arange(w_in.shape[1]) < e)[None, :]
    w_scaled = jnp.where(q_cols, w_in * (NAT_HEAD_DIM ** -0.5 * LOG2_E), w_in)
    qkvz = _in_proj(x2, g_pre, w_scaled.astype(BF16))
    a = _nat_mixer(qkvz, rpb, b, s)
    return _out_proj(a, w_out.astype(BF16), g_post, x2).reshape(b, s, d)


def _gla_layer(x, g_pre, g_post, w_in, wa1_f, wa2_f, ba_f, wa1_b, wa2_b, ba_b, g_norm, w_out):
    b, s, d = x.shape
    x2 = x.reshape(b * s, d)
    qkvz, g1 = _in_proj(x2, g_pre, w_in.astype(BF16), jnp.concatenate([wa1_f, wa1_b], axis=1))
    a = _gla_mixer(qkvz, g1, wa2_f, ba_f, wa2_b, ba_b, g_norm, b, s)
    return _out_proj(a, w_out.astype(BF16), g_post, x2).reshape(b, s, d)


def _trunk(x, norm_pre_g, norm_post_g, fnet_w_in, fnet_w_out, nat_w_in, nat_rpb, nat_w_out,
           gla_w_in, gla_wa1_f, gla_wa2_f, gla_ba_f, gla_wa1_b, gla_wa2_b, gla_ba_b, gla_g_norm,
           gla_w_out):
    depth = norm_pre_g.shape[0]
    for i in range(depth):
        m, j = i % 3, i // 3
        if m == 0:
            x = _fnet_layer(x, norm_pre_g[i], norm_post_g[i], fnet_w_in[j], fnet_w_out[j])
        elif m == 1:
            x = _nat_layer(x, norm_pre_g[i], norm_post_g[i], nat_w_in[j], nat_rpb[j], nat_w_out[j])
        else:
            x = _gla_layer(x, norm_pre_g[i], norm_post_g[i], gla_w_in[j], gla_wa1_f[j], gla_wa2_f[j],
                           gla_ba_f[j], gla_wa1_b[j], gla_wa2_b[j], gla_ba_b[j], gla_g_norm[j],
                           gla_w_out[j])
    return x


def kernel(x_prompt, x_sample, norm_pre_g, norm_post_g, fnet_w_in, fnet_w_out, nat_w_in, nat_rpb,
           nat_w_out, gla_w_in, gla_wa1_f, gla_wa2_f, gla_ba_f, gla_wa1_b, gla_wa2_b, gla_ba_b,
           gla_g_norm, gla_w_out):
    params = (norm_pre_g, norm_post_g, fnet_w_in, fnet_w_out, nat_w_in, nat_rpb, nat_w_out,
              gla_w_in, gla_wa1_f, gla_wa2_f, gla_ba_f, gla_wa1_b, gla_wa2_b, gla_ba_b, gla_g_norm,
              gla_w_out)
    return (_trunk(x_prompt, *params), _trunk(x_sample, *params))
```

```python
import functools
import math

import numpy as np
import jax
import jax.numpy as jnp
from jax import lax
from jax.experimental import pallas as pl
from jax.experimental.pallas import tpu as pltpu

F32 = jnp.float32
BF16 = jnp.bfloat16

RMS_EPS = 1e-6
GRID_W = 64
FNET_GROUP_W = 512
DFT_N2 = 128
NAT_HEAD_DIM = 128
NAT_WIN_H = 8
NAT_WIN_W = 16
NAT_Q_ROWS = 8
NAT_KEY_SUB = 4
NAT_N_SUB = 4
NAT_HEAD_GROUP = 4
GLA_HEADS = 4
GLA_GATE_RANK = 16
GLA_GATE_TEMP = 16.0
GLA_CHUNK = 64
GLA_SUPER = 256
GLA_HEADS_PER_STEP = 2
NEG_MASK = -1e30
LOG2_E = 1.4426950408889634

V7X_VMEM_LIMIT_BYTES = 58 * 1024 * 1024


def _params(*sem):
    return pltpu.CompilerParams(dimension_semantics=sem, vmem_limit_bytes=V7X_VMEM_LIMIT_BYTES)


def _silu(z):
    return z / (1.0 + jnp.exp(-z))


def _dot(a, b):
    return jnp.dot(a, b, preferred_element_type=F32)


def _dot_nt(a, b):
    return lax.dot_general(a, b, (((1,), (1,)), ((), ())), preferred_element_type=F32)


def _dot_tn(a, b):
    return lax.dot_general(a, b, (((0,), (0,)), ((), ())), preferred_element_type=F32)


def _split_bf16(a):
    hi = a.astype(BF16)
    lo = (a - hi.astype(F32)).astype(BF16)
    return hi, lo


def _in_proj_kernel(*refs, has_aux, row_chunk):
    if has_aux:
        x_ref, g_ref, w_ref, wa_ref, o_ref, aux_ref, h_ref = refs
    else:
        x_ref, g_ref, w_ref, o_ref, h_ref = refs

    @pl.when(pl.program_id(1) == 0)
    def _():
        def body(c, carry):
            r = pl.multiple_of(c * row_chunk, row_chunk)
            x = x_ref[pl.ds(r, row_chunk), :]
            ms = jnp.mean(x * x, axis=-1, keepdims=True)
            hn = x * lax.rsqrt(ms + RMS_EPS) * g_ref[...]
            h_ref[pl.ds(r, row_chunk), :] = hn.astype(BF16)
            if has_aux:
                h_hi, h_lo = _split_bf16(hn)
                w_hi = wa_ref[0]
                w_lo = wa_ref[1]
                aux_ref[pl.ds(r, row_chunk), :] = _dot(h_hi, w_hi) + _dot(h_lo, w_hi) + _dot(h_hi, w_lo)
            return carry
        lax.fori_loop(0, x_ref.shape[0] // row_chunk, body, 0)

    o_ref[...] = _dot(h_ref[...], w_ref[...]).astype(o_ref.dtype)


def _in_proj(x2d, g, w_bf16, w_aux=None):
    t, d = x2d.shape
    n = w_bf16.shape[1]
    tm = min(1024, t)
    tn = min(2048, n)
    assert t % tm == 0 and n % tn == 0 and tm % 128 == 0
    has_aux = w_aux is not None
    in_specs = [
        pl.BlockSpec((tm, d), lambda i, j: (i, 0)),
        pl.BlockSpec((1, d), lambda i, j: (0, 0)),
        pl.BlockSpec((d, tn), lambda i, j: (0, j)),
    ]
    args = [x2d, g.reshape(1, d), w_bf16]
    out_shape = [jax.ShapeDtypeStruct((t, n), BF16)]
    out_specs = [pl.BlockSpec((tm, tn), lambda i, j: (i, j))]
    if has_aux:
        na = w_aux.shape[1]
        hi, lo = _split_bf16(w_aux)
        in_specs.append(pl.BlockSpec((2, d, na), lambda i, j: (0, 0, 0)))
        args.append(jnp.stack([hi, lo]))
        out_shape.append(jax.ShapeDtypeStruct((t, na), F32))
        out_specs.append(pl.BlockSpec((tm, na), lambda i, j: (i, 0)))
    res = pl.pallas_call(
        functools.partial(_in_proj_kernel, has_aux=has_aux, row_chunk=128),
        grid=(t // tm, n // tn),
        in_specs=in_specs,
        out_specs=out_specs,
        out_shape=out_shape,
        scratch_shapes=[pltpu.VMEM((tm, d), BF16)],
        compiler_params=_params("parallel", "arbitrary"),
    )(*args)
    return res if has_aux else res[0]


def _out_proj_kernel(a_ref, w_ref, g_ref, x_ref, o_ref):
    t = _dot(a_ref[...], w_ref[...])
    ms = jnp.mean(t * t, axis=-1, keepdims=True)
    o_ref[...] = x_ref[...] + t * lax.rsqrt(ms + RMS_EPS) * g_ref[...]


def _out_proj(a2d, w_bf16, g, x2d):
    t, e = a2d.shape
    d = w_bf16.shape[1]
    tm = min(512, t)
    assert t % tm == 0
    return pl.pallas_call(
        _out_proj_kernel,
        grid=(t // tm,),
        in_specs=[pl.BlockSpec((tm, e), lambda i: (i, 0)),
                  pl.BlockSpec((e, d), lambda i: (0, 0), pipeline_mode=pl.Buffered(1)),
                  pl.BlockSpec((1, d), lambda i: (0, 0)),
                  pl.BlockSpec((tm, d), lambda i: (i, 0))],
        out_specs=pl.BlockSpec((tm, d), lambda i: (i, 0)),
        out_shape=jax.ShapeDtypeStruct((t, d), F32),
        compiler_params=_params("parallel"),
    )(a2d, w_bf16, g.reshape(1, d), x2d)


def _gated_out_proj_kernel(*refs, n_z, head_norm_width):
    m_ref = refs[0]
    z_refs = refs[1:1 + n_z]
    if head_norm_width:
        gn_ref, w_ref, g_ref, x_ref, o_ref = refs[1 + n_z:]
    else:
        w_ref, g_ref, x_ref, o_ref = refs[1 + n_z:]
    e = m_ref.shape[1]
    zw = e // n_z
    cw = head_norm_width if head_norm_width else min(zw, 1024)
    t = None
    for c in range(e // cw):
        sl = slice(c * cw, (c + 1) * cw)
        m = m_ref[:, sl].astype(F32)
        if head_norm_width:
            m = m * lax.rsqrt(jnp.mean(m * m, axis=-1, keepdims=True) + RMS_EPS) * gn_ref[...]
        zi, zo = divmod(c * cw, zw)
        z = z_refs[zi][:, zo:zo + cw].astype(F32)
        part = _dot((m * _silu(z)).astype(BF16), w_ref[sl, :])
        t = part if t is None else t + part
    ms = jnp.mean(t * t, axis=-1, keepdims=True)
    o_ref[...] = x_ref[...] + t * lax.rsqrt(ms + RMS_EPS) * g_ref[...]


def _gated_out_proj(m2d, proj, z_start, w_bf16, g, x2d, head_gain=None):
    t, e = m2d.shape
    d = w_bf16.shape[1]
    tm = min(256, t)
    n_z = 2
    zw = e // n_z
    assert t % tm == 0 and z_start % zw == 0
    zb = z_start // zw
    in_specs = [pl.BlockSpec((tm, e), lambda i: (i, 0))]
    in_specs += [pl.BlockSpec((tm, zw), lambda i, j=j: (i, zb + j)) for j in range(n_z)]
    args = [m2d] + [proj] * n_z
    hw = 0
    if head_gain is not None:
        hw = head_gain.shape[0]
        assert zw % hw == 0
        in_specs.append(pl.BlockSpec((1, hw), lambda i: (0, 0)))
        args.append(head_gain.reshape(1, hw).astype(F32))
    in_specs += [pl.BlockSpec((e, d), lambda i: (0, 0), pipeline_mode=pl.Buffered(1)),
                 pl.BlockSpec((1, d), lambda i: (0, 0)),
                 pl.BlockSpec((tm, d), lambda i: (i, 0))]
    args += [w_bf16, g.reshape(1, d), x2d]
    return pl.pallas_call(
        functools.partial(_gated_out_proj_kernel, n_z=n_z, head_norm_width=hw),
        grid=(t // tm,),
        in_specs=in_specs,
        out_specs=pl.BlockSpec((tm, d), lambda i: (i, 0)),
        out_shape=jax.ShapeDtypeStruct((t, d), F32),
        compiler_params=_params("parallel"),
    )(*args)


def _dft_tables(n1, gb):
    n2 = DFT_N2
    n = n1 * n2
    k1 = np.arange(n1)
    ang_a = 2.0 * np.pi * np.outer(k1, k1) / n1
    eye = np.eye(gb)
    fa_re = np.kron(eye, np.cos(ang_a))
    fa_im = np.kron(eye, -np.sin(ang_a))
    fa = np.concatenate([fa_re, fa_im], axis=0) / math.sqrt(n1)

    kk1 = jnp.arange(n1, dtype=jnp.int32)[:, None, None]
    kk2 = jnp.arange(n2, dtype=jnp.int32)[None, :, None]
    nn2 = jnp.arange(n2, dtype=jnp.int32)[None, None, :]
    m = (nn2 * (kk1 + n1 * kk2)) % n
    ang = m.astype(F32) * (2.0 * math.pi / n)
    c = jnp.cos(ang) / math.sqrt(n2)
    s = jnp.sin(ang) / math.sqrt(n2)
    mtab = jnp.concatenate([jnp.concatenate([c, s], axis=2), jnp.concatenate([-s, c], axis=2)], axis=1)

    cw = FNET_GROUP_W
    ang_c = 2.0 * np.pi * (np.outer(np.arange(cw), np.arange(cw)) % cw) / cw
    cc = np.cos(ang_c) / math.sqrt(cw)
    sc = np.sin(ang_c) / math.sqrt(cw)
    return (jnp.asarray(fa, F32).astype(BF16), mtab.astype(BF16),
            jnp.asarray(cc, F32).astype(BF16), jnp.asarray(sc, F32).astype(BF16))


def _dft_a_kernel(f_ref, u_ref, o_ref):
    o_ref[...] = _dot(f_ref[...], u_ref[...]).astype(o_ref.dtype)


def _dft_c_kernel(a_ref, m_ref, cc_ref, sc_ref, z_ref, o_ref, zr_ref, zi_ref, *, rb, cb):
    n2 = DFT_N2
    for j in range(rb):
        zz = _dot(m_ref[j], a_ref[j])
        zr_ref[j * n2:(j + 1) * n2, :] = zz[:n2].astype(BF16)
        zi_ref[j * n2:(j + 1) * n2, :] = zz[n2:].astype(BF16)
    cw = FNET_GROUP_W
    for gi in range(cb // cw):
        sl = slice(gi * cw, (gi + 1) * cw)
        y = _dot(zr_ref[:, sl], cc_ref[...]) + _dot(zi_ref[:, sl], sc_ref[...])
        z = z_ref[:, sl].astype(F32)
        o_ref[:, sl] = (y * _silu(z)).astype(o_ref.dtype)


def _fnet_mixer(uz, b, s):
    e = uz.shape[1] // 2
    n2 = DFT_N2
    assert s % n2 == 0
    n1 = s // n2
    gb = max(1, min(b, 256 // n1))
    while b % gb:
        gb -= 1
    rg = gb * n1
    ng = b // gb
    fa, mtab, cc, sc = _dft_tables(n1, gb)

    u_t = jnp.transpose(uz.reshape(b, n1, n2, 2 * e)[..., :e], (2, 0, 1, 3)).reshape(n2, ng, rg, e)
    ca = min(4096, e)
    a_nat = pl.pallas_call(
        _dft_a_kernel,
        grid=(n2, ng, e // ca),
        in_specs=[pl.BlockSpec((2 * rg, rg), lambda n, g, c: (0, 0)),
                  pl.BlockSpec((None, None, rg, ca), lambda n, g, c: (n, g, 0, c))],
        out_specs=pl.BlockSpec((None, None, 2 * rg, ca), lambda n, g, c: (n, g, 0, c)),
        out_shape=jax.ShapeDtypeStruct((n2, ng, 2 * rg, e), BF16),
        compiler_params=_params("parallel", "parallel", "parallel"),
    )(fa, u_t)

    r = b * n1
    a3 = jnp.transpose(a_nat.reshape(n2, ng, 2, rg, e), (1, 3, 2, 0, 4)).reshape(r, 2 * n2, e)
    z_t = jnp.transpose(uz.reshape(b, n2, n1, 2 * e)[..., e:], (0, 2, 1, 3)).reshape(r * n2, e)

    rb = min(8, n1)
    assert n1 % rb == 0
    cb = min(1024, e)
    nkb = n1 // rb
    a_t = pl.pallas_call(
        functools.partial(_dft_c_kernel, rb=rb, cb=cb),
        grid=(r // rb, e // cb),
        in_specs=[pl.BlockSpec((rb, 2 * n2, cb), lambda i, c: (i, 0, c)),
                  pl.BlockSpec((rb, 2 * n2, 2 * n2), lambda i, c: (i % nkb, 0, 0)),
                  pl.BlockSpec((FNET_GROUP_W, FNET_GROUP_W), lambda i, c: (0, 0)),
                  pl.BlockSpec((FNET_GROUP_W, FNET_GROUP_W), lambda i, c: (0, 0)),
                  pl.BlockSpec((rb * n2, cb), lambda i, c: (i, c))],
        out_specs=pl.BlockSpec((rb * n2, cb), lambda i, c: (i, c)),
        out_shape=jax.ShapeDtypeStruct((r * n2, e), BF16),
        scratch_shapes=[pltpu.VMEM((rb * n2, cb), BF16), pltpu.VMEM((rb * n2, cb), BF16)],
        compiler_params=_params("parallel", "arbitrary"),
    )(a3, mtab, cc, sc, z_t)
    return jnp.transpose(a_t.reshape(b, n1, n2, e), (0, 2, 1, 3)).reshape(b * s, e)


def _nat_bias_pairs(rpb):
    qc = np.arange(GRID_W)[:, None]
    kc = np.arange(GRID_W)[None, :]
    ws = np.clip(qc - NAT_WIN_W // 2, 0, GRID_W - NAT_WIN_W)
    valid = (kc >= ws) & (kc < ws + NAT_WIN_W)
    rel = np.clip(kc - qc, -(NAT_WIN_W - 1), NAT_WIN_W - 1) + NAT_WIN_W - 1
    colb = jnp.where(jnp.asarray(valid)[None, None], rpb[:, :, jnp.asarray(rel)].astype(F32), NEG_MASK)
    return jnp.concatenate([colb[:, :-1], colb[:, 1:]], axis=-1) * LOG2_E


def _nat_kernel(*refs, n_rb, heads):
    q_ref = refs[0]
    k_refs = refs[1:1 + NAT_N_SUB]
    v_refs = refs[1 + NAT_N_SUB:1 + 2 * NAT_N_SUB]
    bias_ref, o_ref, s_ref, p_ref = refs[1 + 2 * NAT_N_SUB:]
    hd = NAT_HEAD_DIM
    w = GRID_W
    nq = NAT_Q_ROWS * w
    nks = NAT_KEY_SUB * w
    lane = lax.broadcasted_iota(jnp.int32, (w, 2 * w), 1)

    rows_half = NAT_Q_ROWS // 2
    hq = rows_half * w
    subs_half = NAT_N_SUB - 1
    pairs_sub = NAT_KEY_SUB // 2

    def run(window_starts):
        def scores(h, slot):
            off = pl.multiple_of(h * hd, hd)
            ks = [k_refs[j][:, :, pl.ds(off, hd)].reshape(nks, hd) for j in range(NAT_N_SUB)]
            for a in range(2):
                q = q_ref[a * rows_half:(a + 1) * rows_half, :, pl.ds(off, hd)].reshape(hq, hd)
                for j in range(a, a + subs_half):
                    s_ref[slot, a * hq:(a + 1) * hq, j * nks:(j + 1) * nks] = _dot_nt(q, ks[j])

        def softmax(h, slot):
            for i in range(NAT_Q_ROWS):
                lo = window_starts[i]
                kp_lo, kp_hi = lo // 2, (lo + NAT_WIN_H - 1) // 2
                a = i // rows_half
                assert a * pairs_sub <= kp_lo and kp_hi < (a + subs_half) * pairs_sub
                rows = slice(i * w, (i + 1) * w)
                tiles = []
                for kp in range(kp_lo, kp_hi + 1):
                    t = s_ref[slot, rows, kp * 2 * w:(kp + 1) * 2 * w] + bias_ref[h, 2 * kp - i + 3]
                    if 2 * kp < lo:
                        t = jnp.where(lane >= w, t, NEG_MASK)
                    if 2 * kp + 1 >= lo + NAT_WIN_H:
                        t = jnp.where(lane < w, t, NEG_MASK)
                    tiles.append(t)
                m = tiles[0]
                for t in tiles[1:]:
                    m = jnp.maximum(m, t)
                m = jnp.max(m, axis=-1, keepdims=True)
                es = [jnp.exp2(t - m) for t in tiles]
                tot = es[0]
                for ee in es[1:]:
                    tot = tot + ee
                inv = 1.0 / jnp.sum(tot, axis=-1, keepdims=True)
                for kp in range(a * pairs_sub, (a + subs_half) * pairs_sub):
                    cols = slice(kp * 2 * w, (kp + 1) * 2 * w)
                    if kp_lo <= kp <= kp_hi:
                        p_ref[slot, rows, cols] = (es[kp - kp_lo] * inv).astype(BF16)
                    else:
                        p_ref[slot, rows, cols] = jnp.zeros((w, 2 * w), BF16)

        def weighted_sum(h, slot):
            off = pl.multiple_of(h * hd, hd)
            vs = [v_refs[j][:, :, pl.ds(off, hd)].reshape(nks, hd) for j in range(NAT_N_SUB)]
            for a in range(2):
                acc = None
                for j in range(a, a + subs_half):
                    part = _dot(p_ref[slot, a * hq:(a + 1) * hq, j * nks:(j + 1) * nks], vs[j])
                    acc = part if acc is None else acc + part
                qrows = slice(a * rows_half, (a + 1) * rows_half)
                o_ref[qrows, :, pl.ds(off, hd)] = acc.astype(o_ref.dtype).reshape(rows_half, w, hd)

        def group_body(t, carry):
            hs = [NAT_HEAD_GROUP * t + g for g in range(NAT_HEAD_GROUP)]
            for g, h in enumerate(hs):
                scores(h, g)
            for g, h in enumerate(hs):
                softmax(h, g)
            for g, h in enumerate(hs):
                weighted_sum(h, g)
            return carry
        lax.fori_loop(0, heads // NAT_HEAD_GROUP, group_body, 0)

    rb = pl.program_id(2)
    half = NAT_WIN_H // 2
    interior = list(range(NAT_Q_ROWS))
    top = [max(i, half) for i in range(NAT_Q_ROWS)]
    bottom = [min(i, half) for i in range(NAT_Q_ROWS)]

    @pl.when(rb == 0)
    def _():
        run(top)

    @pl.when(rb == n_rb - 1)
    def _():
        run(bottom)

    @pl.when(jnp.logical_and(rb > 0, rb < n_rb - 1))
    def _():
        run(interior)


def _nat_mixer(qkvz, rpb, b, s):
    e = qkvz.shape[1] // 4
    w = GRID_W
    rows = s // w
    assert s % w == 0 and rows % NAT_Q_ROWS == 0 and rows >= 2 * NAT_Q_ROWS
    n_rb = rows // NAT_Q_ROWS
    hb = 8
    lw = hb * NAT_HEAD_DIM
    nhg = e // lw
    x4 = qkvz.reshape(b, rows, w, 4 * e)
    bias = _nat_bias_pairs(rpb)
    n_kblk = rows // NAT_KEY_SUB

    def kv_spec(j, sec):
        def imap(g, bi, r):
            blk = jnp.clip(2 * r - 1 + j, 0, n_kblk - 1)
            return (bi, blk, 0, sec * nhg + g)
        return pl.BlockSpec((None, NAT_KEY_SUB, w, lw), imap)

    in_specs = ([pl.BlockSpec((None, NAT_Q_ROWS, w, lw), lambda g, bi, r: (bi, r, 0, g))]
                + [kv_spec(j, 1) for j in range(NAT_N_SUB)]
                + [kv_spec(j, 2) for j in range(NAT_N_SUB)]
                + [pl.BlockSpec((hb, 2 * NAT_WIN_H - 2, w, 2 * w), lambda g, bi, r: (g, 0, 0, 0))])
    nq = NAT_Q_ROWS * w
    nk = NAT_N_SUB * NAT_KEY_SUB * w
    out = pl.pallas_call(
        functools.partial(_nat_kernel, n_rb=n_rb, heads=hb),
        grid=(nhg, b, n_rb),
        in_specs=in_specs,
        out_specs=pl.BlockSpec((None, NAT_Q_ROWS, w, lw), lambda g, bi, r: (bi, r, 0, g)),
        out_shape=jax.ShapeDtypeStruct((b, rows, w, e), BF16),
        scratch_shapes=[pltpu.VMEM((NAT_HEAD_GROUP, nq, nk), F32),
                        pltpu.VMEM((NAT_HEAD_GROUP, nq, nk), BF16)],
        compiler_params=_params("parallel", "parallel", "arbitrary"),
    )(*([x4] * (1 + 2 * NAT_N_SUB)), bias)
    return out.reshape(b * s, e)


def _log_gate(g1, wa2_ref, ba_ref):
    g_hi, g_lo = _split_bf16(g1)
    w_hi = wa2_ref[0]
    w_lo = wa2_ref[1]
    pre = _dot(g_hi, w_hi) + _dot(g_lo, w_hi) + _dot(g_hi, w_lo) + ba_ref[...]
    return (jnp.minimum(pre, 0.0) - jnp.log(1.0 + jnp.exp(-jnp.abs(pre)))) / GLA_GATE_TEMP


def _gla_superchunk(q, k, v, la, state_ref, reverse):
    n, dk = q.shape
    c = GLA_CHUNK
    nb = n // c
    ii = lax.broadcasted_iota(jnp.int32, (n, n), 0)
    jj = lax.broadcasted_iota(jnp.int32, (n, n), 1)
    tri = jnp.where((jj >= ii) if reverse else (jj <= ii), 1.0, 0.0).astype(BF16)
    la_hi, la_lo = _split_bf16(la)
    bc = _dot(tri, la_hi) + _dot(tri, la_lo)
    qf = q.astype(F32) * (dk ** -0.5)
    kf = k.astype(F32)

    def brow(idx):
        return bc[idx:idx + 1]

    ka_parts = []
    mids = []
    for blk in range(nb):
        r0 = blk * c
        b_mid = brow(r0 + c // 2) if reverse else brow(r0 + c // 2 - 1)
        mids.append(b_mid)
        ka_parts.append((kf[r0:r0 + c] * jnp.exp(b_mid - bc[r0:r0 + c])).astype(BF16))
    ka = jnp.concatenate(ka_parts, axis=0)

    row_i = lax.broadcasted_iota(jnp.int32, (c, n), 0)
    col_j = lax.broadcasted_iota(jnp.int32, (c, n), 1)
    s_rows = []
    for blk in range(nb):
        r0 = blk * c
        b_blk = bc[r0:r0 + c]
        q_blk = qf[r0:r0 + c]
        qa = (q_blk * jnp.exp(b_blk - mids[blk])).astype(BF16)
        s_diag = _dot_nt(qa, ka)
        in_blk = jnp.logical_and(col_j >= r0, col_j < r0 + c)
        if reverse:
            keep = jnp.logical_and(in_blk, col_j > row_i + r0)
        else:
            keep = jnp.logical_and(in_blk, col_j <= row_i + r0)
        s_blk = jnp.where(keep, s_diag, 0.0)
        has_other = blk < nb - 1 if reverse else blk > 0
        if has_other:
            b_s = brow(r0 + c) if reverse else brow(r0 - 1)
            qo = (q_blk * jnp.exp(b_blk - b_s)).astype(BF16)
            if reverse:
                ko = (kf[r0 + c:] * jnp.exp(b_s - bc[r0 + c:])).astype(BF16)
                ko = jnp.concatenate([jnp.zeros((r0 + c, dk), BF16), ko], axis=0)
            else:
                ko = (kf[:r0] * jnp.exp(b_s - bc[:r0])).astype(BF16)
                ko = jnp.concatenate([ko, jnp.zeros((n - r0, dk), BF16)], axis=0)
            s_blk = s_blk + _dot_nt(qo, ko)
        s_rows.append(s_blk.astype(BF16))
    scores = jnp.concatenate(s_rows, axis=0)

    b_end = brow(0) if reverse else brow(n - 1)
    q_in = (qf * jnp.exp(bc)).astype(BF16)
    o = _dot(scores, v) + _dot(q_in, state_ref[...].astype(BF16))
    k_out = (kf * jnp.exp(b_end - bc)).astype(BF16)
    ones = jnp.ones((n, 128), BF16)
    decay = jnp.exp(_dot_tn(la_hi, ones) + _dot_tn(la_lo, ones))
    upd = _dot_tn(k_out, v)
    for lb in range(v.shape[1] // 128):
        sl = slice(lb * 128, (lb + 1) * 128)
        state_ref[:, sl] = decay * state_ref[:, sl] + upd[:, sl]
    return o


def _gla_fwd_kernel(q_ref, k_ref, v_ref, g1_ref, wa2_ref, ba_ref, o_ref, state_ref):
    @pl.when(pl.program_id(2) == 0)
    def _():
        state_ref[...] = jnp.zeros_like(state_ref)

    n_heads, dk, dv = state_ref.shape
    la = _log_gate(g1_ref[:, :GLA_GATE_RANK], wa2_ref, ba_ref)
    n = GLA_SUPER
    for sc in range(q_ref.shape[0] // n):
        rows = slice(sc * n, (sc + 1) * n)
        for g in range(n_heads):
            kl = slice(g * dk, (g + 1) * dk)
            vl = slice(g * dv, (g + 1) * dv)
            o = _gla_superchunk(q_ref[rows, kl], k_ref[rows, kl], v_ref[rows, vl], la[rows, kl],
                                state_ref.at[g], False)
            o_ref[rows, vl] = o.astype(o_ref.dtype)


def _gla_bwd_kernel(q_ref, k_ref, v_ref, g1_ref, wa2_ref, ba_ref, of_ref, o_ref, state_ref):
    @pl.when(pl.program_id(2) == 0)
    def _():
        state_ref[...] = jnp.zeros_like(state_ref)

    n_heads, dk, dv = state_ref.shape
    la = _log_gate(g1_ref[:, GLA_GATE_RANK:], wa2_ref, ba_ref)
    n = GLA_SUPER
    for sc in reversed(range(q_ref.shape[0] // n)):
        rows = slice(sc * n, (sc + 1) * n)
        for g in range(n_heads):
            kl = slice(g * dk, (g + 1) * dk)
            vl = slice(g * dv, (g + 1) * dv)
            o = _gla_superchunk(q_ref[rows, kl], k_ref[rows, kl], v_ref[rows, vl], la[rows, kl],
                                state_ref.at[g], True)
            o_ref[rows, vl] = (o + of_ref[rows, vl].astype(F32)).astype(o_ref.dtype)


def _gla_mixer(qkvz, g1, wa2_f, ba_f, wa2_b, ba_b, g_norm, b, s):
    hh = GLA_HEADS
    dv = g_norm.shape[0]
    e = hh * dv
    kd = (qkvz.shape[1] - 2 * e) // 2
    dk = kd // hh
    hp = GLA_HEADS_PER_STEP
    wk, wv = hp * dk, hp * dv
    assert hh % hp == 0 and kd % wk == 0 and (2 * kd) % wv == 0
    tb = min(512, s)
    assert s % tb == 0 and tb % GLA_SUPER == 0
    nt = s // tb
    x3 = qkvz.reshape(b, s, qkvz.shape[1])
    g3 = g1.reshape(b, s, 2 * GLA_GATE_RANK)
    k0 = kd // wk
    v0 = 2 * kd // wv

    def specs(tmap):
        return [pl.BlockSpec((None, tb, wk), lambda bi, h, t: (bi, tmap(t), h)),
                pl.BlockSpec((None, tb, wk), lambda bi, h, t: (bi, tmap(t), k0 + h)),
                pl.BlockSpec((None, tb, wv), lambda bi, h, t: (bi, tmap(t), v0 + h)),
                pl.BlockSpec((None, tb, 2 * GLA_GATE_RANK), lambda bi, h, t: (bi, tmap(t), 0)),
                pl.BlockSpec((2, GLA_GATE_RANK, wk), lambda bi, h, t: (0, 0, h)),
                pl.BlockSpec((1, wk), lambda bi, h, t: (0, h))]

    def gate_w(wa2):
        hi, lo = _split_bf16(wa2)
        return jnp.stack([hi, lo])

    fwd = lambda t: t
    o_f = pl.pallas_call(
        _gla_fwd_kernel,
        grid=(b, hh // hp, nt),
        in_specs=specs(fwd),
        out_specs=pl.BlockSpec((None, tb, wv), lambda bi, h, t: (bi, t, h)),
        out_shape=jax.ShapeDtypeStruct((b, s, e), BF16),
        scratch_shapes=[pltpu.VMEM((hp, dk, dv), F32)],
        compiler_params=_params("parallel", "parallel", "arbitrary"),
    )(x3, x3, x3, g3, gate_w(wa2_f), ba_f.reshape(1, kd))

    rev = lambda t: nt - 1 - t
    out = pl.pallas_call(
        _gla_bwd_kernel,
        grid=(b, hh // hp, nt),
        in_specs=specs(rev) + [pl.BlockSpec((None, tb, wv), lambda bi, h, t: (bi, rev(t), h))],
        out_specs=pl.BlockSpec((None, tb, wv), lambda bi, h, t: (bi, rev(t), h)),
        out_shape=jax.ShapeDtypeStruct((b, s, e), BF16),
        scratch_shapes=[pltpu.VMEM((hp, dk, dv), F32)],
        compiler_params=_params("parallel", "parallel", "arbitrary"),
    )(x3, x3, x3, g3, gate_w(wa2_b), ba_b.reshape(1, kd), o_f)
    return out.reshape(b * s, e)


def _fnet_layer(x, g_pre, g_post, w_in, w_out):
    b, s, d = x.shape
    x2 = x.reshape(b * s, d)
    uz = _in_proj(x2, g_pre, w_in.astype(BF16))
    a = _fnet_mixer(uz, b, s)
    return _out_proj(a, w_out.astype(BF16), g_post, x2).reshape(b, s, d)


def _nat_layer(x, g_pre, g_post, w_in, rpb, w_out):
    b, s, d = x.shape
    x2 = x.reshape(b * s, d)
    e = w_in.shape[1] // 4
    q_cols = (jnp.arange(w_in.shape[1]) < e)[None, :]
    w_scaled = jnp.where(q_cols, w_in * (NAT_HEAD_DIM ** -0.5 * LOG2_E), w_in)
    qkvz = _in_proj(x2, g_pre, w_scaled.astype(BF16))
    m = _nat_mixer(qkvz, rpb, b, s)
    return _gated_out_proj(m, qkvz, 3 * e, w_out.astype(BF16), g_post, x2).reshape(b, s, d)


def _gla_layer(x, g_pre, g_post, w_in, wa1_f, wa2_f, ba_f, wa1_b, wa2_b, ba_b, g_norm, w_out):
    b, s, d = x.shape
    x2 = x.reshape(b * s, d)
    qkvz, g1 = _in_proj(x2, g_pre, w_in.astype(BF16), jnp.concatenate([wa1_f, wa1_b], axis=1))
    m = _gla_mixer(qkvz, g1, wa2_f, ba_f, wa2_b, ba_b, g_norm, b, s)
    z_start = qkvz.shape[1] - m.shape[1]
    return _gated_out_proj(m, qkvz, z_start, w_out.astype(BF16), g_post, x2,
                           head_gain=g_norm).reshape(b, s, d)


def _trunk(x, norm_pre_g, norm_post_g, fnet_w_in, fnet_w_out, nat_w_in, nat_rpb, nat_w_out,
           gla_w_in, gla_wa1_f, gla_wa2_f, gla_ba_f, gla_wa1_b, gla_wa2_b, gla_ba_b, gla_g_norm,
           gla_w_out):
    depth = norm_pre_g.shape[0]
    for i in range(depth):
        m, j = i % 3, i // 3
        if m == 0:
            x = _fnet_layer(x, norm_pre_g[i], norm_post_g[i], fnet_w_in[j], fnet_w_out[j])
        elif m == 1:
            x = _nat_layer(x, norm_pre_g[i], norm_post_g[i], nat_w_in[j], nat_rpb[j], nat_w_out[j])
        else:
            x = _gla_layer(x, norm_pre_g[i], norm_post_g[i], gla_w_in[j], gla_wa1_f[j], gla_wa2_f[j],
                           gla_ba_f[j], gla_wa1_b[j], gla_wa2_b[j], gla_ba_b[j], gla_g_norm[j],
                           gla_w_out[j])
    return x


def kernel(x_prompt, x_sample, norm_pre_g, norm_post_g, fnet_w_in, fnet_w_out, nat_w_in, nat_rpb,
           nat_w_out, gla_w_in, gla_wa1_f, gla_wa2_f, gla_ba_f, gla_wa1_b, gla_wa2_b, gla_ba_b,
           gla_g_norm, gla_w_out):
    params = (norm_pre_g, norm_post_g, fnet_w_in, fnet_w_out, nat_w_in, nat_rpb, nat_w_out,
              gla_w_in, gla_wa1_f, gla_wa2_f, gla_ba_f, gla_wa1_b, gla_wa2_b, gla_ba_b, gla_g_norm,
              gla_w_out)
    return (_trunk(x_prompt, *params), _trunk(x_sample, *params))
```

```python
import functools
import math

import numpy as np
import jax
import jax.numpy as jnp
from jax import lax
from jax.experimental import pallas as pl
from jax.experimental.pallas import tpu as pltpu

F32 = jnp.float32
BF16 = jnp.bfloat16

RMS_EPS = 1e-6
GRID_W = 64
FNET_GROUP_W = 512
DFT_N2 = 128
NAT_HEAD_DIM = 128
NAT_WIN_H = 8
NAT_WIN_W = 16
NAT_Q_ROWS = 8
NAT_KEY_SUB = 4
NAT_N_SUB = 4
NAT_HEAD_GROUP = 4
GLA_HEADS = 4
GLA_GATE_RANK = 16
GLA_GATE_TEMP = 16.0
GLA_CHUNK = 64
GLA_SUPER = 256
GLA_HEADS_PER_STEP = 2
NEG_MASK = -1e30
LOG2_E = 1.4426950408889634

V7X_VMEM_LIMIT_BYTES = 58 * 1024 * 1024


def _params(*sem):
    return pltpu.CompilerParams(dimension_semantics=sem, vmem_limit_bytes=V7X_VMEM_LIMIT_BYTES)


def _silu(z):
    return z / (1.0 + jnp.exp(-z))


def _dot(a, b):
    return jnp.dot(a, b, preferred_element_type=F32)


def _dot_nt(a, b):
    return lax.dot_general(a, b, (((1,), (1,)), ((), ())), preferred_element_type=F32)


def _dot_tn(a, b):
    return lax.dot_general(a, b, (((0,), (0,)), ((), ())), preferred_element_type=F32)


def _split_bf16(a):
    hi = a.astype(BF16)
    lo = (a - hi.astype(F32)).astype(BF16)
    return hi, lo


def _in_proj_kernel(*refs, has_aux, row_chunk):
    if has_aux:
        x_ref, g_ref, w_ref, wa_ref, o_ref, aux_ref, h_ref = refs
    else:
        x_ref, g_ref, w_ref, o_ref, h_ref = refs

    @pl.when(pl.program_id(1) == 0)
    def _():
        def body(c, carry):
            r = pl.multiple_of(c * row_chunk, row_chunk)
            x = x_ref[pl.ds(r, row_chunk), :]
            ms = jnp.mean(x * x, axis=-1, keepdims=True)
            hn = x * lax.rsqrt(ms + RMS_EPS) * g_ref[...]
            h_ref[pl.ds(r, row_chunk), :] = hn.astype(BF16)
            if has_aux:
                h_hi, h_lo = _split_bf16(hn)
                w_hi = wa_ref[0]
                w_lo = wa_ref[1]
                aux_ref[pl.ds(r, row_chunk), :] = _dot(h_hi, w_hi) + _dot(h_lo, w_hi) + _dot(h_hi, w_lo)
            return carry
        lax.fori_loop(0, x_ref.shape[0] // row_chunk, body, 0)

    o_ref[...] = _dot(h_ref[...], w_ref[...]).astype(o_ref.dtype)


def _in_proj(x2d, g, w_bf16, w_aux=None):
    t, d = x2d.shape
    n = w_bf16.shape[1]
    tm = min(1024, t)
    tn = min(2048, n)
    assert t % tm == 0 and n % tn == 0 and tm % 128 == 0
    has_aux = w_aux is not None
    in_specs = [
        pl.BlockSpec((tm, d), lambda i, j: (i, 0)),
        pl.BlockSpec((1, d), lambda i, j: (0, 0)),
        pl.BlockSpec((d, tn), lambda i, j: (0, j)),
    ]
    args = [x2d, g.reshape(1, d), w_bf16]
    out_shape = [jax.ShapeDtypeStruct((t, n), BF16)]
    out_specs = [pl.BlockSpec((tm, tn), lambda i, j: (i, j))]
    if has_aux:
        na = w_aux.shape[1]
        hi, lo = _split_bf16(w_aux)
        in_specs.append(pl.BlockSpec((2, d, na), lambda i, j: (0, 0, 0)))
        args.append(jnp.stack([hi, lo]))
        out_shape.append(jax.ShapeDtypeStruct((t, na), F32))
        out_specs.append(pl.BlockSpec((tm, na), lambda i, j: (i, 0)))
    res = pl.pallas_call(
        functools.partial(_in_proj_kernel, has_aux=has_aux, row_chunk=128),
        grid=(t // tm, n // tn),
        in_specs=in_specs,
        out_specs=out_specs,
        out_shape=out_shape,
        scratch_shapes=[pltpu.VMEM((tm, d), BF16)],
        compiler_params=_params("parallel", "arbitrary"),
    )(*args)
    return res if has_aux else res[0]


def _out_proj_kernel(a_ref, w_ref, g_ref, x_ref, o_ref):
    t = _dot(a_ref[...], w_ref[...])
    ms = jnp.mean(t * t, axis=-1, keepdims=True)
    o_ref[...] = x_ref[...] + t * lax.rsqrt(ms + RMS_EPS) * g_ref[...]


def _out_proj(a2d, w_bf16, g, x2d):
    t, e = a2d.shape
    d = w_bf16.shape[1]
    tm = min(512, t)
    assert t % tm == 0
    return pl.pallas_call(
        _out_proj_kernel,
        grid=(t // tm,),
        in_specs=[pl.BlockSpec((tm, e), lambda i: (i, 0)),
                  pl.BlockSpec((e, d), lambda i: (0, 0), pipeline_mode=pl.Buffered(1)),
                  pl.BlockSpec((1, d), lambda i: (0, 0)),
                  pl.BlockSpec((tm, d), lambda i: (i, 0))],
        out_specs=pl.BlockSpec((tm, d), lambda i: (i, 0)),
        out_shape=jax.ShapeDtypeStruct((t, d), F32),
        compiler_params=_params("parallel"),
    )(a2d, w_bf16, g.reshape(1, d), x2d)


def _gated_out_proj_kernel(*refs, n_z, head_norm_width):
    m_ref = refs[0]
    z_refs = refs[1:1 + n_z]
    if head_norm_width:
        gn_ref, w_ref, g_ref, x_ref, o_ref = refs[1 + n_z:]
    else:
        w_ref, g_ref, x_ref, o_ref = refs[1 + n_z:]
    e = m_ref.shape[1]
    zw = e // n_z
    cw = head_norm_width if head_norm_width else min(zw, 1024)
    t = None
    for c in range(e // cw):
        sl = slice(c * cw, (c + 1) * cw)
        m = m_ref[:, sl].astype(F32)
        if head_norm_width:
            m = m * lax.rsqrt(jnp.mean(m * m, axis=-1, keepdims=True) + RMS_EPS) * gn_ref[...]
        zi, zo = divmod(c * cw, zw)
        z = z_refs[zi][:, zo:zo + cw].astype(F32)
        part = _dot((m * _silu(z)).astype(BF16), w_ref[sl, :])
        t = part if t is None else t + part
    ms = jnp.mean(t * t, axis=-1, keepdims=True)
    o_ref[...] = x_ref[...] + t * lax.rsqrt(ms + RMS_EPS) * g_ref[...]


def _gated_out_proj(m2d, proj, z_start, w_bf16, g, x2d, head_gain=None):
    t, e = m2d.shape
    d = w_bf16.shape[1]
    tm = min(256, t)
    n_z = 2
    zw = e // n_z
    assert t % tm == 0 and z_start % zw == 0
    zb = z_start // zw
    in_specs = [pl.BlockSpec((tm, e), lambda i: (i, 0))]
    in_specs += [pl.BlockSpec((tm, zw), lambda i, j=j: (i, zb + j)) for j in range(n_z)]
    args = [m2d] + [proj] * n_z
    hw = 0
    if head_gain is not None:
        hw = head_gain.shape[0]
        assert zw % hw == 0
        in_specs.append(pl.BlockSpec((1, hw), lambda i: (0, 0)))
        args.append(head_gain.reshape(1, hw).astype(F32))
    in_specs += [pl.BlockSpec((e, d), lambda i: (0, 0), pipeline_mode=pl.Buffered(1)),
                 pl.BlockSpec((1, d), lambda i: (0, 0)),
                 pl.BlockSpec((tm, d), lambda i: (i, 0))]
    args += [w_bf16, g.reshape(1, d), x2d]
    return pl.pallas_call(
        functools.partial(_gated_out_proj_kernel, n_z=n_z, head_norm_width=hw),
        grid=(t // tm,),
        in_specs=in_specs,
        out_specs=pl.BlockSpec((tm, d), lambda i: (i, 0)),
        out_shape=jax.ShapeDtypeStruct((t, d), F32),
        compiler_params=_params("parallel"),
    )(*args)


def _dft_tables(n1, gb):
    n2 = DFT_N2
    n = n1 * n2
    k1 = np.arange(n1)
    ang_a = 2.0 * np.pi * np.outer(k1, k1) / n1
    eye = np.eye(gb)
    fa_re = np.kron(eye, np.cos(ang_a))
    fa_im = np.kron(eye, -np.sin(ang_a))
    fa = np.concatenate([fa_re, fa_im], axis=0) / math.sqrt(n1)

    kk1 = jnp.arange(n1, dtype=jnp.int32)[:, None, None]
    kk2 = jnp.arange(n2, dtype=jnp.int32)[None, :, None]
    nn2 = jnp.arange(n2, dtype=jnp.int32)[None, None, :]
    m = (nn2 * (kk1 + n1 * kk2)) % n
    ang = m.astype(F32) * (2.0 * math.pi / n)
    c = jnp.cos(ang) / math.sqrt(n2)
    s = jnp.sin(ang) / math.sqrt(n2)
    mtab = jnp.concatenate([jnp.concatenate([c, s], axis=2), jnp.concatenate([-s, c], axis=2)], axis=1)

    cw = FNET_GROUP_W
    ang_c = 2.0 * np.pi * (np.outer(np.arange(cw), np.arange(cw)) % cw) / cw
    cc = np.cos(ang_c)[:, :cw // 2] / math.sqrt(cw)
    sc = np.sin(ang_c)[:, :cw // 2] / math.sqrt(cw)
    return (jnp.asarray(fa, F32).astype(BF16), mtab.astype(BF16),
            jnp.asarray(cc, F32).astype(BF16), jnp.asarray(sc, F32).astype(BF16))


def _dft_a_kernel(f_ref, u_ref, o_ref):
    o_ref[...] = _dot(f_ref[...], u_ref[...]).astype(o_ref.dtype)


def _dft_c_kernel(a_ref, m_ref, cc_ref, sc_ref, z_ref, o_ref, zr_ref, zi_ref, *, rb, cb):
    n2 = DFT_N2
    for j in range(rb):
        zz = _dot(m_ref[j], a_ref[j])
        zr_ref[j * n2:(j + 1) * n2, :] = zz[:n2].astype(BF16)
        zi_ref[j * n2:(j + 1) * n2, :] = zz[n2:].astype(BF16)
    cw = FNET_GROUP_W
    hw = cw // 2
    lane = lax.broadcasted_iota(jnp.int32, (1, cw), 1)
    alt_sign = jnp.where(lane % 2 == 0, 1.0, -1.0) * (cw ** -0.5)
    first = lax.broadcasted_iota(jnp.int32, (1, hw), 1) == 0
    for gi in range(cb // cw):
        sl = slice(gi * cw, (gi + 1) * cw)
        zr = zr_ref[:, sl]
        p = _dot(zr, cc_ref[...])
        q = _dot(zi_ref[:, sl], sc_ref[...])
        y_mid = jnp.sum(zr.astype(F32) * alt_sign, axis=-1, keepdims=True)
        y = jnp.concatenate([p + q, jnp.where(first, y_mid, p - q)], axis=-1)
        z = z_ref[:, sl].astype(F32)
        o_ref[:, sl] = (y * _silu(z)).astype(o_ref.dtype)


def _fnet_channel_perm(e):
    cw = FNET_GROUP_W
    within = np.concatenate([np.arange(cw // 2), [cw // 2], np.arange(cw - 1, cw // 2, -1)])
    return (np.arange(0, e, cw)[:, None] + within[None, :]).reshape(-1)


def _fnet_mixer(uz, b, s):
    e = uz.shape[1] // 2
    n2 = DFT_N2
    assert s % n2 == 0
    n1 = s // n2
    gb = max(1, min(b, 256 // n1))
    while b % gb:
        gb -= 1
    rg = gb * n1
    ng = b // gb
    fa, mtab, cc, sc = _dft_tables(n1, gb)

    u_t = jnp.transpose(uz.reshape(b, n1, n2, 2 * e)[..., :e], (2, 0, 1, 3)).reshape(n2, ng, rg, e)
    ca = min(4096, e)
    a_nat = pl.pallas_call(
        _dft_a_kernel,
        grid=(n2, ng, e // ca),
        in_specs=[pl.BlockSpec((2 * rg, rg), lambda n, g, c: (0, 0)),
                  pl.BlockSpec((None, None, rg, ca), lambda n, g, c: (n, g, 0, c))],
        out_specs=pl.BlockSpec((None, None, 2 * rg, ca), lambda n, g, c: (n, g, 0, c)),
        out_shape=jax.ShapeDtypeStruct((n2, ng, 2 * rg, e), BF16),
        compiler_params=_params("parallel", "parallel", "parallel"),
    )(fa, u_t)

    r = b * n1
    a3 = jnp.transpose(a_nat.reshape(n2, ng, 2, rg, e), (1, 3, 2, 0, 4)).reshape(r, 2 * n2, e)
    z_t = jnp.transpose(uz.reshape(b, n2, n1, 2 * e)[..., e:], (0, 2, 1, 3)).reshape(r * n2, e)

    rb = min(8, n1)
    assert n1 % rb == 0
    cb = min(1024, e)
    nkb = n1 // rb
    a_t = pl.pallas_call(
        functools.partial(_dft_c_kernel, rb=rb, cb=cb),
        grid=(r // rb, e // cb),
        in_specs=[pl.BlockSpec((rb, 2 * n2, cb), lambda i, c: (i, 0, c)),
                  pl.BlockSpec((rb, 2 * n2, 2 * n2), lambda i, c: (i % nkb, 0, 0)),
                  pl.BlockSpec((FNET_GROUP_W, FNET_GROUP_W // 2), lambda i, c: (0, 0)),
                  pl.BlockSpec((FNET_GROUP_W, FNET_GROUP_W // 2), lambda i, c: (0, 0)),
                  pl.BlockSpec((rb * n2, cb), lambda i, c: (i, c))],
        out_specs=pl.BlockSpec((rb * n2, cb), lambda i, c: (i, c)),
        out_shape=jax.ShapeDtypeStruct((r * n2, e), BF16),
        scratch_shapes=[pltpu.VMEM((rb * n2, cb), BF16), pltpu.VMEM((rb * n2, cb), BF16)],
        compiler_params=_params("parallel", "arbitrary"),
    )(a3, mtab, cc, sc, z_t)
    return jnp.transpose(a_t.reshape(b, n1, n2, e), (0, 2, 1, 3)).reshape(b * s, e)


def _nat_bias_pairs(rpb):
    qc = np.arange(GRID_W)[:, None]
    kc = np.arange(GRID_W)[None, :]
    ws = np.clip(qc - NAT_WIN_W // 2, 0, GRID_W - NAT_WIN_W)
    valid = (kc >= ws) & (kc < ws + NAT_WIN_W)
    rel = np.clip(kc - qc, -(NAT_WIN_W - 1), NAT_WIN_W - 1) + NAT_WIN_W - 1
    colb = jnp.where(jnp.asarray(valid)[None, None], rpb[:, :, jnp.asarray(rel)].astype(F32), NEG_MASK)
    return jnp.concatenate([colb[:, :-1], colb[:, 1:]], axis=-1) * LOG2_E


def _nat_kernel(*refs, n_rb, heads):
    q_ref = refs[0]
    k_refs = refs[1:1 + NAT_N_SUB]
    v_refs = refs[1 + NAT_N_SUB:1 + 2 * NAT_N_SUB]
    bias_ref, o_ref, s_ref, p_ref = refs[1 + 2 * NAT_N_SUB:]
    hd = NAT_HEAD_DIM
    w = GRID_W
    nq = NAT_Q_ROWS * w
    nks = NAT_KEY_SUB * w
    lane = lax.broadcasted_iota(jnp.int32, (w, 2 * w), 1)

    rows_half = NAT_Q_ROWS // 2
    hq = rows_half * w
    subs_half = NAT_N_SUB - 1
    pairs_sub = NAT_KEY_SUB // 2

    def run(window_starts):
        def scores(h, slot):
            off = pl.multiple_of(h * hd, hd)
            ks = [k_refs[j][:, :, pl.ds(off, hd)].reshape(nks, hd) for j in range(NAT_N_SUB)]
            for a in range(2):
                q = q_ref[a * rows_half:(a + 1) * rows_half, :, pl.ds(off, hd)].reshape(hq, hd)
                for j in range(a, a + subs_half):
                    s_ref[slot, a * hq:(a + 1) * hq, j * nks:(j + 1) * nks] = _dot_nt(q, ks[j])

        def softmax(h, slot):
            for i in range(NAT_Q_ROWS):
                lo = window_starts[i]
                kp_lo, kp_hi = lo // 2, (lo + NAT_WIN_H - 1) // 2
                a = i // rows_half
                assert a * pairs_sub <= kp_lo and kp_hi < (a + subs_half) * pairs_sub
                rows = slice(i * w, (i + 1) * w)
                tiles = []
                for kp in range(kp_lo, kp_hi + 1):
                    t = s_ref[slot, rows, kp * 2 * w:(kp + 1) * 2 * w] + bias_ref[h, 2 * kp - i + 3]
                    if 2 * kp < lo:
                        t = jnp.where(lane >= w, t, NEG_MASK)
                    if 2 * kp + 1 >= lo + NAT_WIN_H:
                        t = jnp.where(lane < w, t, NEG_MASK)
                    tiles.append(t)
                m = tiles[0]
                for t in tiles[1:]:
                    m = jnp.maximum(m, t)
                m = jnp.max(m, axis=-1, keepdims=True)
                es = [jnp.exp2(t - m) for t in tiles]
                tot = es[0]
                for ee in es[1:]:
                    tot = tot + ee
                inv = 1.0 / jnp.sum(tot, axis=-1, keepdims=True)
                for kp in range(a * pairs_sub, (a + subs_half) * pairs_sub):
                    cols = slice(kp * 2 * w, (kp + 1) * 2 * w)
                    if kp_lo <= kp <= kp_hi:
                        p_ref[slot, rows, cols] = (es[kp - kp_lo] * inv).astype(BF16)
                    else:
                        p_ref[slot, rows, cols] = jnp.zeros((w, 2 * w), BF16)

        def weighted_sum(h, slot):
            off = pl.multiple_of(h * hd, hd)
            vs = [v_refs[j][:, :, pl.ds(off, hd)].reshape(nks, hd) for j in range(NAT_N_SUB)]
            for a in range(2):
                acc = None
                for j in range(a, a + subs_half):
                    part = _dot(p_ref[slot, a * hq:(a + 1) * hq, j * nks:(j + 1) * nks], vs[j])
                    acc = part if acc is None else acc + part
                qrows = slice(a * rows_half, (a + 1) * rows_half)
                o_ref[qrows, :, pl.ds(off, hd)] = acc.astype(o_ref.dtype).reshape(rows_half, w, hd)

        def group_body(t, carry):
            hs = [NAT_HEAD_GROUP * t + g for g in range(NAT_HEAD_GROUP)]
            for g, h in enumerate(hs):
                scores(h, g)
            for g, h in enumerate(hs):
                softmax(h, g)
            for g, h in enumerate(hs):
                weighted_sum(h, g)
            return carry
        lax.fori_loop(0, heads // NAT_HEAD_GROUP, group_body, 0)

    rb = pl.program_id(2)
    half = NAT_WIN_H // 2
    interior = list(range(NAT_Q_ROWS))
    top = [max(i, half) for i in range(NAT_Q_ROWS)]
    bottom = [min(i, half) for i in range(NAT_Q_ROWS)]

    @pl.when(rb == 0)
    def _():
        run(top)

    @pl.when(rb == n_rb - 1)
    def _():
        run(bottom)

    @pl.when(jnp.logical_and(rb > 0, rb < n_rb - 1))
    def _():
        run(interior)


def _nat_mixer(qkvz, rpb, b, s):
    e = qkvz.shape[1] // 4
    w = GRID_W
    rows = s // w
    assert s % w == 0 and rows % NAT_Q_ROWS == 0 and rows >= 2 * NAT_Q_ROWS
    n_rb = rows // NAT_Q_ROWS
    hb = 8
    lw = hb * NAT_HEAD_DIM
    nhg = e // lw
    x4 = qkvz.reshape(b, rows, w, 4 * e)
    bias = _nat_bias_pairs(rpb)
    n_kblk = rows // NAT_KEY_SUB

    def kv_spec(j, sec):
        def imap(g, bi, r):
            blk = jnp.clip(2 * r - 1 + j, 0, n_kblk - 1)
            return (bi, blk, 0, sec * nhg + g)
        return pl.BlockSpec((None, NAT_KEY_SUB, w, lw), imap)

    in_specs = ([pl.BlockSpec((None, NAT_Q_ROWS, w, lw), lambda g, bi, r: (bi, r, 0, g))]
                + [kv_spec(j, 1) for j in range(NAT_N_SUB)]
                + [kv_spec(j, 2) for j in range(NAT_N_SUB)]
                + [pl.BlockSpec((hb, 2 * NAT_WIN_H - 2, w, 2 * w), lambda g, bi, r: (g, 0, 0, 0))])
    nq = NAT_Q_ROWS * w
    nk = NAT_N_SUB * NAT_KEY_SUB * w
    out = pl.pallas_call(
        functools.partial(_nat_kernel, n_rb=n_rb, heads=hb),
        grid=(nhg, b, n_rb),
        in_specs=in_specs,
        out_specs=pl.BlockSpec((None, NAT_Q_ROWS, w, lw), lambda g, bi, r: (bi, r, 0, g)),
        out_shape=jax.ShapeDtypeStruct((b, rows, w, e), BF16),
        scratch_shapes=[pltpu.VMEM((NAT_HEAD_GROUP, nq, nk), F32),
                        pltpu.VMEM((NAT_HEAD_GROUP, nq, nk), BF16)],
        compiler_params=_params("parallel", "parallel", "arbitrary"),
    )(*([x4] * (1 + 2 * NAT_N_SUB)), bias)
    return out.reshape(b * s, e)


def _log_gate(g1, wa2_ref, ba_ref):
    g_hi, g_lo = _split_bf16(g1)
    w_hi = wa2_ref[0]
    w_lo = wa2_ref[1]
    pre = _dot(g_hi, w_hi) + _dot(g_lo, w_hi) + _dot(g_hi, w_lo) + ba_ref[...]
    return (jnp.minimum(pre, 0.0) - jnp.log(1.0 + jnp.exp(-jnp.abs(pre)))) / GLA_GATE_TEMP


def _gla_superchunk(q, k, v, la, state_ref, reverse):
    n, dk = q.shape
    c = GLA_CHUNK
    nb = n // c
    ii = lax.broadcasted_iota(jnp.int32, (n, n), 0)
    jj = lax.broadcasted_iota(jnp.int32, (n, n), 1)
    tri = jnp.where((jj >= ii) if reverse else (jj <= ii), 1.0, 0.0).astype(BF16)
    la_hi, la_lo = _split_bf16(la)
    bc = _dot(tri, la_hi) + _dot(tri, la_lo)
    qf = q.astype(F32) * (dk ** -0.5)
    kf = k.astype(F32)

    def brow(idx):
        return bc[idx:idx + 1]

    ka_parts = []
    mids = []
    for blk in range(nb):
        r0 = blk * c
        b_mid = brow(r0 + c // 2) if reverse else brow(r0 + c // 2 - 1)
        mids.append(b_mid)
        ka_parts.append((kf[r0:r0 + c] * jnp.exp(b_mid - bc[r0:r0 + c])).astype(BF16))
    ka = jnp.concatenate(ka_parts, axis=0)

    row_i = lax.broadcasted_iota(jnp.int32, (c, n), 0)
    col_j = lax.broadcasted_iota(jnp.int32, (c, n), 1)
    s_rows = []
    for blk in range(nb):
        r0 = blk * c
        b_blk = bc[r0:r0 + c]
        q_blk = qf[r0:r0 + c]
        qa = (q_blk * jnp.exp(b_blk - mids[blk])).astype(BF16)
        s_diag = _dot_nt(qa, ka)
        in_blk = jnp.logical_and(col_j >= r0, col_j < r0 + c)
        if reverse:
            keep = jnp.logical_and(in_blk, col_j > row_i + r0)
        else:
            keep = jnp.logical_and(in_blk, col_j <= row_i + r0)
        s_blk = jnp.where(keep, s_diag, 0.0)
        has_other = blk < nb - 1 if reverse else blk > 0
        if has_other:
            b_s = brow(r0 + c) if reverse else brow(r0 - 1)
            qo = (q_blk * jnp.exp(b_blk - b_s)).astype(BF16)
            if reverse:
                ko = (kf[r0 + c:] * jnp.exp(b_s - bc[r0 + c:])).astype(BF16)
                ko = jnp.concatenate([jnp.zeros((r0 + c, dk), BF16), ko], axis=0)
            else:
                ko = (kf[:r0] * jnp.exp(b_s - bc[:r0])).astype(BF16)
                ko = jnp.concatenate([ko, jnp.zeros((n - r0, dk), BF16)], axis=0)
            s_blk = s_blk + _dot_nt(qo, ko)
        s_rows.append(s_blk.astype(BF16))
    scores = jnp.concatenate(s_rows, axis=0)

    b_end = brow(0) if reverse else brow(n - 1)
    q_in = (qf * jnp.exp(bc)).astype(BF16)
    o = _dot(scores, v) + _dot(q_in, state_ref[...].astype(BF16))
    k_out = (kf * jnp.exp(b_end - bc)).astype(BF16)
    ones = jnp.ones((n, 128), BF16)
    decay = jnp.exp(_dot_tn(la_hi, ones) + _dot_tn(la_lo, ones))
    upd = _dot_tn(k_out, v)
    for lb in range(v.shape[1] // 128):
        sl = slice(lb * 128, (lb + 1) * 128)
        state_ref[:, sl] = decay * state_ref[:, sl] + upd[:, sl]
    return o


def _gla_fwd_kernel(q_ref, k_ref, v_ref, g1_ref, wa2_ref, ba_ref, o_ref, state_ref):
    @pl.when(pl.program_id(2) == 0)
    def _():
        state_ref[...] = jnp.zeros_like(state_ref)

    n_heads, dk, dv = state_ref.shape
    la = _log_gate(g1_ref[:, :GLA_GATE_RANK], wa2_ref, ba_ref)
    n = GLA_SUPER
    for sc in range(q_ref.shape[0] // n):
        rows = slice(sc * n, (sc + 1) * n)
        for g in range(n_heads):
            kl = slice(g * dk, (g + 1) * dk)
            vl = slice(g * dv, (g + 1) * dv)
            o = _gla_superchunk(q_ref[rows, kl], k_ref[rows, kl], v_ref[rows, vl], la[rows, kl],
                                state_ref.at[g], False)
            o_ref[rows, vl] = o.astype(o_ref.dtype)


def _gla_bwd_kernel(q_ref, k_ref, v_ref, g1_ref, wa2_ref, ba_ref, of_ref, o_ref, state_ref):
    @pl.when(pl.program_id(2) == 0)
    def _():
        state_ref[...] = jnp.zeros_like(state_ref)

    n_heads, dk, dv = state_ref.shape
    la = _log_gate(g1_ref[:, GLA_GATE_RANK:], wa2_ref, ba_ref)
    n = GLA_SUPER
    for sc in reversed(range(q_ref.shape[0] // n)):
        rows = slice(sc * n, (sc + 1) * n)
        for g in range(n_heads):
            kl = slice(g * dk, (g + 1) * dk)
            vl = slice(g * dv, (g + 1) * dv)
            o = _gla_superchunk(q_ref[rows, kl], k_ref[rows, kl], v_ref[rows, vl], la[rows, kl],
                                state_ref.at[g], True)
            o_ref[rows, vl] = (o + of_ref[rows, vl].astype(F32)).astype(o_ref.dtype)


def _gla_mixer(qkvz, g1, wa2_f, ba_f, wa2_b, ba_b, g_norm, b, s):
    hh = GLA_HEADS
    dv = g_norm.shape[0]
    e = hh * dv
    kd = (qkvz.shape[1] - 2 * e) // 2
    dk = kd // hh
    hp = GLA_HEADS_PER_STEP
    wk, wv = hp * dk, hp * dv
    assert hh % hp == 0 and kd % wk == 0 and (2 * kd) % wv == 0
    tb = min(512, s)
    assert s % tb == 0 and tb % GLA_SUPER == 0
    nt = s // tb
    x3 = qkvz.reshape(b, s, qkvz.shape[1])
    g3 = g1.reshape(b, s, 2 * GLA_GATE_RANK)
    k0 = kd // wk
    v0 = 2 * kd // wv

    def specs(tmap):
        return [pl.BlockSpec((None, tb, wk), lambda bi, h, t: (bi, tmap(t), h)),
                pl.BlockSpec((None, tb, wk), lambda bi, h, t: (bi, tmap(t), k0 + h)),
                pl.BlockSpec((None, tb, wv), lambda bi, h, t: (bi, tmap(t), v0 + h)),
                pl.BlockSpec((None, tb, 2 * GLA_GATE_RANK), lambda bi, h, t: (bi, tmap(t), 0)),
                pl.BlockSpec((2, GLA_GATE_RANK, wk), lambda bi, h, t: (0, 0, h)),
                pl.BlockSpec((1, wk), lambda bi, h, t: (0, h))]

    def gate_w(wa2):
        hi, lo = _split_bf16(wa2)
        return jnp.stack([hi, lo])

    fwd = lambda t: t
    o_f = pl.pallas_call(
        _gla_fwd_kernel,
        grid=(b, hh // hp, nt),
        in_specs=specs(fwd),
        out_specs=pl.BlockSpec((None, tb, wv), lambda bi, h, t: (bi, t, h)),
        out_shape=jax.ShapeDtypeStruct((b, s, e), BF16),
        scratch_shapes=[pltpu.VMEM((hp, dk, dv), F32)],
        compiler_params=_params("parallel", "parallel", "arbitrary"),
    )(x3, x3, x3, g3, gate_w(wa2_f), ba_f.reshape(1, kd))

    rev = lambda t: nt - 1 - t
    out = pl.pallas_call(
        _gla_bwd_kernel,
        grid=(b, hh // hp, nt),
        in_specs=specs(rev) + [pl.BlockSpec((None, tb, wv), lambda bi, h, t: (bi, rev(t), h))],
        out_specs=pl.BlockSpec((None, tb, wv), lambda bi, h, t: (bi, rev(t), h)),
        out_shape=jax.ShapeDtypeStruct((b, s, e), BF16),
        scratch_shapes=[pltpu.VMEM((hp, dk, dv), F32)],
        compiler_params=_params("parallel", "parallel", "arbitrary"),
    )(x3, x3, x3, g3, gate_w(wa2_b), ba_b.reshape(1, kd), o_f)
    return out.reshape(b * s, e)


def _fnet_layer(x, g_pre, g_post, w_in, w_out):
    b, s, d = x.shape
    x2 = x.reshape(b * s, d)
    e = w_out.shape[0]
    perm = _fnet_channel_perm(e)
    w_uz = jnp.concatenate([w_in[:, :e], w_in[:, e:][:, perm]], axis=1)
    uz = _in_proj(x2, g_pre, w_uz.astype(BF16))
    a = _fnet_mixer(uz, b, s)
    return _out_proj(a, w_out[perm, :].astype(BF16), g_post, x2).reshape(b, s, d)


def _nat_layer(x, g_pre, g_post, w_in, rpb, w_out):
    b, s, d = x.shape
    x2 = x.reshape(b * s, d)
    e = w_in.shape[1] // 4
    q_cols = (jnp.arange(w_in.shape[1]) < e)[None, :]
    w_scaled = jnp.where(q_cols, w_in * (NAT_HEAD_DIM ** -0.5 * LOG2_E), w_in)
    qkvz = _in_proj(x2, g_pre, w_scaled.astype(BF16))
    m = _nat_mixer(qkvz, rpb, b, s)
    return _gated_out_proj(m, qkvz, 3 * e, w_out.astype(BF16), g_post, x2).reshape(b, s, d)


def _gla_layer(x, g_pre, g_post, w_in, wa1_f, wa2_f, ba_f, wa1_b, wa2_b, ba_b, g_norm, w_out):
    b, s, d = x.shape
    x2 = x.reshape(b * s, d)
    qkvz, g1 = _in_proj(x2, g_pre, w_in.astype(BF16), jnp.concatenate([wa1_f, wa1_b], axis=1))
    m = _gla_mixer(qkvz, g1, wa2_f, ba_f, wa2_b, ba_b, g_norm, b, s)
    z_start = qkvz.shape[1] - m.shape[1]
    return _gated_out_proj(m, qkvz, z_start, w_out.astype(BF16), g_post, x2,
                           head_gain=g_norm).reshape(b, s, d)


def _trunk(x, norm_pre_g, norm_post_g, fnet_w_in, fnet_w_out, nat_w_in, nat_rpb, nat_w_out,
           gla_w_in, gla_wa1_f, gla_wa2_f, gla_ba_f, gla_wa1_b, gla_wa2_b, gla_ba_b, gla_g_norm,
           gla_w_out):
    depth = norm_pre_g.shape[0]
    for i in range(depth):
        m, j = i % 3, i // 3
        if m == 0:
            x = _fnet_layer(x, norm_pre_g[i], norm_post_g[i], fnet_w_in[j], fnet_w_out[j])
        elif m == 1:
            x = _nat_layer(x, norm_pre_g[i], norm_post_g[i], nat_w_in[j], nat_rpb[j], nat_w_out[j])
        else:
            x = _gla_layer(x, norm_pre_g[i], norm_post_g[i], gla_w_in[j], gla_wa1_f[j], gla_wa2_f[j],
                           gla_ba_f[j], gla_wa1_b[j], gla_wa2_b[j], gla_ba_b[j], gla_g_norm[j],
                           gla_w_out[j])
    return x


def kernel(x_prompt, x_sample, norm_pre_g, norm_post_g, fnet_w_in, fnet_w_out, nat_w_in, nat_rpb,
           nat_w_out, gla_w_in, gla_wa1_f, gla_wa2_f, gla_ba_f, gla_wa1_b, gla_wa2_b, gla_ba_b,
           gla_g_norm, gla_w_out):
    params = (norm_pre_g, norm_post_g, fnet_w_in, fnet_w_out, nat_w_in, nat_rpb, nat_w_out,
              gla_w_in, gla_wa1_f, gla_wa2_f, gla_ba_f, gla_wa1_b, gla_wa2_b, gla_ba_b, gla_g_norm,
              gla_w_out)
    return (_trunk(x_prompt, *params), _trunk(x_sample, *params))
```

```python
import functools
import math

import numpy as np
import jax
import jax.numpy as jnp
from jax import lax
from jax.experimental import pallas as pl
from jax.experimental.pallas import tpu as pltpu

F32 = jnp.float32
BF16 = jnp.bfloat16

RMS_EPS = 1e-6
GRID_W = 64
FNET_GROUP_W = 512
DFT_N2 = 128
NAT_HEAD_DIM = 128
NAT_WIN_H = 8
NAT_WIN_W = 16
NAT_Q_ROWS = 8
NAT_KEY_SUB = 4
NAT_N_SUB = 4
NAT_HEAD_GROUP = 4
GLA_HEADS = 4
GLA_GATE_RANK = 16
GLA_GATE_TEMP = 16.0
GLA_CHUNK = 64
GLA_SUPER = 256
GLA_HEADS_PER_STEP = 2
NEG_MASK = -1e30
LOG2_E = 1.4426950408889634

V7X_VMEM_LIMIT_BYTES = 58 * 1024 * 1024


def _params(*sem):
    return pltpu.CompilerParams(dimension_semantics=sem, vmem_limit_bytes=V7X_VMEM_LIMIT_BYTES)


def _silu(z):
    return z / (1.0 + jnp.exp(-z))


def _dot(a, b):
    return jnp.dot(a, b, preferred_element_type=F32)


def _dot_nt(a, b):
    return lax.dot_general(a, b, (((1,), (1,)), ((), ())), preferred_element_type=F32)


def _dot_tn(a, b):
    return lax.dot_general(a, b, (((0,), (0,)), ((), ())), preferred_element_type=F32)


def _split_bf16(a):
    hi = a.astype(BF16)
    lo = (a - hi.astype(F32)).astype(BF16)
    return hi, lo


def _in_proj_kernel(*refs, has_aux, row_chunk):
    if has_aux:
        x_ref, g_ref, w_ref, wa_ref, o_ref, aux_ref, h_ref = refs
    else:
        x_ref, g_ref, w_ref, o_ref, h_ref = refs

    @pl.when(pl.program_id(1) == 0)
    def _():
        def body(c, carry):
            r = pl.multiple_of(c * row_chunk, row_chunk)
            x = x_ref[pl.ds(r, row_chunk), :]
            ms = jnp.mean(x * x, axis=-1, keepdims=True)
            hn = x * lax.rsqrt(ms + RMS_EPS) * g_ref[...]
            h_ref[pl.ds(r, row_chunk), :] = hn.astype(BF16)
            if has_aux:
                h_hi, h_lo = _split_bf16(hn)
                w_hi = wa_ref[0]
                w_lo = wa_ref[1]
                aux_ref[pl.ds(r, row_chunk), :] = _dot(h_hi, w_hi) + _dot(h_lo, w_hi) + _dot(h_hi, w_lo)
            return carry
        lax.fori_loop(0, x_ref.shape[0] // row_chunk, body, 0)

    o_ref[...] = _dot(h_ref[...], w_ref[...]).astype(o_ref.dtype)


def _in_proj(x2d, g, w_bf16, w_aux=None):
    t, d = x2d.shape
    n = w_bf16.shape[1]
    tm = min(1024, t)
    tn = min(2048, n)
    assert t % tm == 0 and n % tn == 0 and tm % 128 == 0
    has_aux = w_aux is not None
    in_specs = [
        pl.BlockSpec((tm, d), lambda i, j: (i, 0)),
        pl.BlockSpec((1, d), lambda i, j: (0, 0)),
        pl.BlockSpec((d, tn), lambda i, j: (0, j)),
    ]
    args = [x2d, g.reshape(1, d), w_bf16]
    out_shape = [jax.ShapeDtypeStruct((t, n), BF16)]
    out_specs = [pl.BlockSpec((tm, tn), lambda i, j: (i, j))]
    if has_aux:
        na = w_aux.shape[1]
        hi, lo = _split_bf16(w_aux)
        in_specs.append(pl.BlockSpec((2, d, na), lambda i, j: (0, 0, 0)))
        args.append(jnp.stack([hi, lo]))
        out_shape.append(jax.ShapeDtypeStruct((t, na), F32))
        out_specs.append(pl.BlockSpec((tm, na), lambda i, j: (i, 0)))
    res = pl.pallas_call(
        functools.partial(_in_proj_kernel, has_aux=has_aux, row_chunk=128),
        grid=(t // tm, n // tn),
        in_specs=in_specs,
        out_specs=out_specs,
        out_shape=out_shape,
        scratch_shapes=[pltpu.VMEM((tm, d), BF16)],
        compiler_params=_params("parallel", "arbitrary"),
    )(*args)
    return res if has_aux else res[0]


def _out_proj_kernel(a_ref, w_ref, g_ref, x_ref, o_ref):
    t = _dot(a_ref[...], w_ref[...])
    ms = jnp.mean(t * t, axis=-1, keepdims=True)
    o_ref[...] = x_ref[...] + t * lax.rsqrt(ms + RMS_EPS) * g_ref[...]


def _out_proj(a2d, w_bf16, g, x2d):
    t, e = a2d.shape
    d = w_bf16.shape[1]
    tm = min(512, t)
    assert t % tm == 0
    return pl.pallas_call(
        _out_proj_kernel,
        grid=(t // tm,),
        in_specs=[pl.BlockSpec((tm, e), lambda i: (i, 0)),
                  pl.BlockSpec((e, d), lambda i: (0, 0), pipeline_mode=pl.Buffered(1)),
                  pl.BlockSpec((1, d), lambda i: (0, 0)),
                  pl.BlockSpec((tm, d), lambda i: (i, 0))],
        out_specs=pl.BlockSpec((tm, d), lambda i: (i, 0)),
        out_shape=jax.ShapeDtypeStruct((t, d), F32),
        compiler_params=_params("parallel"),
    )(a2d, w_bf16, g.reshape(1, d), x2d)


def _gated_out_proj_kernel(*refs, n_z, head_norm_width):
    m_ref = refs[0]
    z_refs = refs[1:1 + n_z]
    if head_norm_width:
        gn_ref, w_ref, g_ref, x_ref, o_ref = refs[1 + n_z:]
    else:
        w_ref, g_ref, x_ref, o_ref = refs[1 + n_z:]
    e = m_ref.shape[1]
    zw = e // n_z
    cw = head_norm_width if head_norm_width else min(zw, 1024)
    t = None
    for c in range(e // cw):
        sl = slice(c * cw, (c + 1) * cw)
        m = m_ref[:, sl].astype(F32)
        if head_norm_width:
            m = m * lax.rsqrt(jnp.mean(m * m, axis=-1, keepdims=True) + RMS_EPS) * gn_ref[...]
        zi, zo = divmod(c * cw, zw)
        z = z_refs[zi][:, zo:zo + cw].astype(F32)
        part = _dot((m * _silu(z)).astype(BF16), w_ref[sl, :])
        t = part if t is None else t + part
    ms = jnp.mean(t * t, axis=-1, keepdims=True)
    o_ref[...] = x_ref[...] + t * lax.rsqrt(ms + RMS_EPS) * g_ref[...]


def _gated_out_proj(m2d, proj, z_start, w_bf16, g, x2d, head_gain=None):
    t, e = m2d.shape
    d = w_bf16.shape[1]
    tm = min(256, t)
    n_z = 2
    zw = e // n_z
    assert t % tm == 0 and z_start % zw == 0
    zb = z_start // zw
    in_specs = [pl.BlockSpec((tm, e), lambda i: (i, 0))]
    in_specs += [pl.BlockSpec((tm, zw), lambda i, j=j: (i, zb + j)) for j in range(n_z)]
    args = [m2d] + [proj] * n_z
    hw = 0
    if head_gain is not None:
        hw = head_gain.shape[0]
        assert zw % hw == 0
        in_specs.append(pl.BlockSpec((1, hw), lambda i: (0, 0)))
        args.append(head_gain.reshape(1, hw).astype(F32))
    in_specs += [pl.BlockSpec((e, d), lambda i: (0, 0), pipeline_mode=pl.Buffered(1)),
                 pl.BlockSpec((1, d), lambda i: (0, 0)),
                 pl.BlockSpec((tm, d), lambda i: (i, 0))]
    args += [w_bf16, g.reshape(1, d), x2d]
    return pl.pallas_call(
        functools.partial(_gated_out_proj_kernel, n_z=n_z, head_norm_width=hw),
        grid=(t // tm,),
        in_specs=in_specs,
        out_specs=pl.BlockSpec((tm, d), lambda i: (i, 0)),
        out_shape=jax.ShapeDtypeStruct((t, d), F32),
        compiler_params=_params("parallel"),
    )(*args)


def _dft_tables(n1, gb, nk):
    n2 = DFT_N2
    n = n1 * n2
    ang_a = 2.0 * np.pi * np.outer(np.arange(nk), np.arange(n1)) / n1
    eye = np.eye(gb)
    fa_re = np.kron(eye, np.cos(ang_a))
    fa_im = np.kron(eye, -np.sin(ang_a))
    fa = np.concatenate([fa_re, fa_im], axis=0) / math.sqrt(n1)

    kk1 = jnp.arange(nk, dtype=jnp.int32)[:, None, None]
    kk2 = jnp.arange(n2, dtype=jnp.int32)[None, :, None]
    nn2 = jnp.arange(n2, dtype=jnp.int32)[None, None, :]
    m = (nn2 * (kk1 + n1 * kk2)) % n
    ang = m.astype(F32) * (2.0 * math.pi / n)
    c = jnp.cos(ang) / math.sqrt(n2)
    s = jnp.sin(ang) / math.sqrt(n2)
    mtab = jnp.concatenate([jnp.concatenate([c, s], axis=2), jnp.concatenate([-s, c], axis=2)], axis=1)

    cw = FNET_GROUP_W
    ang_c = 2.0 * np.pi * (np.outer(np.arange(cw), np.arange(cw)) % cw) / cw
    cc = np.cos(ang_c)[:, :cw // 2] / math.sqrt(cw)
    sc = np.sin(ang_c)[:, :cw // 2] / math.sqrt(cw)
    return (jnp.asarray(fa, F32).astype(BF16), mtab.astype(BF16),
            jnp.asarray(cc, F32).astype(BF16), jnp.asarray(sc, F32).astype(BF16))


def _dft_a_kernel(f_ref, u_ref, o_ref):
    o_ref[...] = _dot(f_ref[...], u_ref[...]).astype(o_ref.dtype)


def _dft_c_kernel(a_ref, m_ref, cc_ref, sc_ref, zp_ref, zm_ref, op_ref, om_ref, zr_ref, zi_ref,
                  *, kb, bb, cb):
    n2 = DFT_N2
    for ki in range(kb):
        for bi in range(bb):
            j = ki * bb + bi
            zz = _dot(m_ref[ki], a_ref[ki, bi])
            zr_ref[j * n2:(j + 1) * n2, :] = zz[:n2].astype(BF16)
            zi_ref[j * n2:(j + 1) * n2, :] = zz[n2:].astype(BF16)
    cw = FNET_GROUP_W
    hw = cw // 2
    rows = kb * bb * n2
    lane = lax.broadcasted_iota(jnp.int32, (1, cw), 1)
    alt_sign = jnp.where(lane % 2 == 0, 1.0, -1.0) * (cw ** -0.5)
    first = lax.broadcasted_iota(jnp.int32, (1, hw), 1) == 0
    for gi in range(cb // cw):
        sl = slice(gi * cw, (gi + 1) * cw)
        zr = zr_ref[:, sl]
        p = _dot(zr, cc_ref[...])
        q = _dot(zi_ref[:, sl], sc_ref[...])
        y_mid = jnp.sum(zr.astype(F32) * alt_sign, axis=-1, keepdims=True)
        ev = p + q
        od = jnp.where(first, y_mid, p - q)
        y_p = jnp.concatenate([ev, od], axis=-1)
        y_m = jnp.concatenate([jnp.where(first, ev, od), jnp.where(first, od, ev)], axis=-1)
        z_p = zp_ref[:, :, :, sl].reshape(rows, cw).astype(F32)
        z_m = zm_ref[:, :, :, sl].reshape(rows, cw).astype(F32)
        op_ref[:, :, :, sl] = (y_p * _silu(z_p)).astype(op_ref.dtype).reshape(kb, bb, n2, cw)
        om_ref[:, :, :, sl] = (y_m * _silu(z_m)).astype(om_ref.dtype).reshape(kb, bb, n2, cw)


def _fnet_permute_channels(w, axis):
    cw = FNET_GROUP_W
    keep = cw // 2 + 1
    shape = w.shape
    g = w.reshape(shape[:axis] + (shape[axis] // cw, cw) + shape[axis + 1:])
    lo = lax.slice_in_dim(g, 0, keep, axis=axis + 1)
    hi = jnp.flip(lax.slice_in_dim(g, keep, cw, axis=axis + 1), axis=axis + 1)
    return jnp.concatenate([lo, hi], axis=axis + 1).reshape(shape)


def _fnet_mixer(uz, b, s):
    e = uz.shape[1] // 2
    n2 = DFT_N2
    assert s % n2 == 0
    n1 = s // n2
    gb = max(1, min(b, 256 // n1))
    while b % gb:
        gb -= 1
    rg = gb * n1
    ng = b // gb
    half = n1 // 2
    assert n1 % 2 == 0 and half >= 2
    nk = half + 1
    if (gb * nk) % 8:
        nk = min(n1, -(-nk // 8) * 8)
    fa, mtab, cc, sc = _dft_tables(n1, gb, nk)

    uz4 = uz.reshape(b, n1, n2, 2 * e)
    u_t = jnp.transpose(uz4[..., :e], (2, 0, 1, 3)).reshape(n2, ng, rg, e)
    ca = min(4096, e)
    ra = 2 * gb * nk
    a_nat = pl.pallas_call(
        _dft_a_kernel,
        grid=(n2, ng, e // ca),
        in_specs=[pl.BlockSpec((ra, rg), lambda n, g, c: (0, 0)),
                  pl.BlockSpec((None, None, rg, ca), lambda n, g, c: (n, g, 0, c))],
        out_specs=pl.BlockSpec((None, None, ra, ca), lambda n, g, c: (n, g, 0, c)),
        out_shape=jax.ShapeDtypeStruct((n2, ng, ra, e), BF16),
        compiler_params=_params("parallel", "parallel", "parallel"),
    )(fa, u_t)

    a3 = jnp.transpose(a_nat.reshape(n2, ng, 2, gb, nk, e), (4, 1, 3, 2, 0, 5)).reshape(nk, b, 2 * n2, e)
    z4 = uz.reshape(b, n2, n1, 2 * e)[..., e:]
    z_p = jnp.transpose(z4[:, :, :nk], (2, 0, 1, 3))
    z_m = jnp.transpose(jnp.roll(jnp.flip(z4, (1, 2)), 1, axis=2)[:, :, :nk], (2, 0, 1, 3))

    bb = min(b, 8)
    while b % bb:
        bb -= 1
    kb = max(1, 8 // bb)
    while nk % kb:
        kb -= 1
    cb = min(1024, e)
    blk = lambda i, j, c: (i, j, 0, c)
    a_p, a_m = pl.pallas_call(
        functools.partial(_dft_c_kernel, kb=kb, bb=bb, cb=cb),
        grid=(nk // kb, b // bb, e // cb),
        in_specs=[pl.BlockSpec((kb, bb, 2 * n2, cb), blk),
                  pl.BlockSpec((kb, 2 * n2, 2 * n2), lambda i, j, c: (i, 0, 0)),
                  pl.BlockSpec((FNET_GROUP_W, FNET_GROUP_W // 2), lambda i, j, c: (0, 0)),
                  pl.BlockSpec((FNET_GROUP_W, FNET_GROUP_W // 2), lambda i, j, c: (0, 0)),
                  pl.BlockSpec((kb, bb, n2, cb), blk),
                  pl.BlockSpec((kb, bb, n2, cb), blk)],
        out_specs=[pl.BlockSpec((kb, bb, n2, cb), blk), pl.BlockSpec((kb, bb, n2, cb), blk)],
        out_shape=[jax.ShapeDtypeStruct((nk, b, n2, e), BF16)] * 2,
        scratch_shapes=[pltpu.VMEM((kb * bb * n2, cb), BF16), pltpu.VMEM((kb * bb * n2, cb), BF16)],
        compiler_params=_params("parallel", "parallel", "arbitrary"),
    )(a3, mtab, cc, sc, z_p, z_m)

    lo = jnp.transpose(a_p[:half + 1], (1, 2, 0, 3))
    hi = jnp.flip(jnp.transpose(a_m[1:half], (1, 2, 0, 3)), (1, 2))
    return jnp.concatenate([lo, hi], axis=2).reshape(b * s, e)


def _nat_bias_pairs(rpb):
    qc = np.arange(GRID_W)[:, None]
    kc = np.arange(GRID_W)[None, :]
    ws = np.clip(qc - NAT_WIN_W // 2, 0, GRID_W - NAT_WIN_W)
    valid = (kc >= ws) & (kc < ws + NAT_WIN_W)
    rel = np.clip(kc - qc, -(NAT_WIN_W - 1), NAT_WIN_W - 1) + NAT_WIN_W - 1
    colb = jnp.where(jnp.asarray(valid)[None, None], rpb[:, :, jnp.asarray(rel)].astype(F32), NEG_MASK)
    return jnp.concatenate([colb[:, :-1], colb[:, 1:]], axis=-1) * LOG2_E


def _nat_kernel(*refs, n_rb, heads):
    q_ref = refs[0]
    k_refs = refs[1:1 + NAT_N_SUB]
    v_refs = refs[1 + NAT_N_SUB:1 + 2 * NAT_N_SUB]
    bias_ref, o_ref, s_ref, p_ref = refs[1 + 2 * NAT_N_SUB:]
    hd = NAT_HEAD_DIM
    w = GRID_W
    nq = NAT_Q_ROWS * w
    nks = NAT_KEY_SUB * w
    lane = lax.broadcasted_iota(jnp.int32, (w, 2 * w), 1)

    rows_half = NAT_Q_ROWS // 2
    hq = rows_half * w
    subs_half = NAT_N_SUB - 1
    pairs_sub = NAT_KEY_SUB // 2

    def run(window_starts):
        def scores(h, slot):
            off = pl.multiple_of(h * hd, hd)
            ks = [k_refs[j][:, :, pl.ds(off, hd)].reshape(nks, hd) for j in range(NAT_N_SUB)]
            for a in range(2):
                q = q_ref[a * rows_half:(a + 1) * rows_half, :, pl.ds(off, hd)].reshape(hq, hd)
                for j in range(a, a + subs_half):
                    s_ref[slot, a * hq:(a + 1) * hq, j * nks:(j + 1) * nks] = _dot_nt(q, ks[j])

        def softmax(h, slot):
            for i in range(NAT_Q_ROWS):
                lo = window_starts[i]
                kp_lo, kp_hi = lo // 2, (lo + NAT_WIN_H - 1) // 2
                a = i // rows_half
                assert a * pairs_sub <= kp_lo and kp_hi < (a + subs_half) * pairs_sub
                rows = slice(i * w, (i + 1) * w)
                tiles = []
                for kp in range(kp_lo, kp_hi + 1):
                    t = s_ref[slot, rows, kp * 2 * w:(kp + 1) * 2 * w] + bias_ref[h, 2 * kp - i + 3]
                    if 2 * kp < lo:
                        t = jnp.where(lane >= w, t, NEG_MASK)
                    if 2 * kp + 1 >= lo + NAT_WIN_H:
                        t = jnp.where(lane < w, t, NEG_MASK)
                    tiles.append(t)
                m = tiles[0]
                for t in tiles[1:]:
                    m = jnp.maximum(m, t)
                m = jnp.max(m, axis=-1, keepdims=True)
                es = [jnp.exp2(t - m) for t in tiles]
                tot = es[0]
                for ee in es[1:]:
                    tot = tot + ee
                inv = 1.0 / jnp.sum(tot, axis=-1, keepdims=True)
                for kp in range(a * pairs_sub, (a + subs_half) * pairs_sub):
                    cols = slice(kp * 2 * w, (kp + 1) * 2 * w)
                    if kp_lo <= kp <= kp_hi:
                        p_ref[slot, rows, cols] = (es[kp - kp_lo] * inv).astype(BF16)
                    else:
                        p_ref[slot, rows, cols] = jnp.zeros((w, 2 * w), BF16)

        def weighted_sum(h, slot):
            off = pl.multiple_of(h * hd, hd)
            vs = [v_refs[j][:, :, pl.ds(off, hd)].reshape(nks, hd) for j in range(NAT_N_SUB)]
            for a in range(2):
                acc = None
                for j in range(a, a + subs_half):
                    part = _dot(p_ref[slot, a * hq:(a + 1) * hq, j * nks:(j + 1) * nks], vs[j])
                    acc = part if acc is None else acc + part
                qrows = slice(a * rows_half, (a + 1) * rows_half)
                o_ref[qrows, :, pl.ds(off, hd)] = acc.astype(o_ref.dtype).reshape(rows_half, w, hd)

        def group_body(t, carry):
            hs = [NAT_HEAD_GROUP * t + g for g in range(NAT_HEAD_GROUP)]
            for g, h in enumerate(hs):
                scores(h, g)
            for g, h in enumerate(hs):
                softmax(h, g)
            for g, h in enumerate(hs):
                weighted_sum(h, g)
            return carry
        lax.fori_loop(0, heads // NAT_HEAD_GROUP, group_body, 0)

    rb = pl.program_id(2)
    half = NAT_WIN_H // 2
    interior = list(range(NAT_Q_ROWS))
    top = [max(i, half) for i in range(NAT_Q_ROWS)]
    bottom = [min(i, half) for i in range(NAT_Q_ROWS)]

    @pl.when(rb == 0)
    def _():
        run(top)

    @pl.when(rb == n_rb - 1)
    def _():
        run(bottom)

    @pl.when(jnp.logical_and(rb > 0, rb < n_rb - 1))
    def _():
        run(interior)


def _nat_mixer(qkvz, rpb, b, s):
    e = qkvz.shape[1] // 4
    w = GRID_W
    rows = s // w
    assert s % w == 0 and rows % NAT_Q_ROWS == 0 and rows >= 2 * NAT_Q_ROWS
    n_rb = rows // NAT_Q_ROWS
    hb = 8
    lw = hb * NAT_HEAD_DIM
    nhg = e // lw
    x4 = qkvz.reshape(b, rows, w, 4 * e)
    bias = _nat_bias_pairs(rpb)
    n_kblk = rows // NAT_KEY_SUB

    def kv_spec(j, sec):
        def imap(g, bi, r):
            blk = jnp.clip(2 * r - 1 + j, 0, n_kblk - 1)
            return (bi, blk, 0, sec * nhg + g)
        return pl.BlockSpec((None, NAT_KEY_SUB, w, lw), imap)

    in_specs = ([pl.BlockSpec((None, NAT_Q_ROWS, w, lw), lambda g, bi, r: (bi, r, 0, g))]
                + [kv_spec(j, 1) for j in range(NAT_N_SUB)]
                + [kv_spec(j, 2) for j in range(NAT_N_SUB)]
                + [pl.BlockSpec((hb, 2 * NAT_WIN_H - 2, w, 2 * w), lambda g, bi, r: (g, 0, 0, 0))])
    nq = NAT_Q_ROWS * w
    nk = NAT_N_SUB * NAT_KEY_SUB * w
    out = pl.pallas_call(
        functools.partial(_nat_kernel, n_rb=n_rb, heads=hb),
        grid=(nhg, b, n_rb),
        in_specs=in_specs,
        out_specs=pl.BlockSpec((None, NAT_Q_ROWS, w, lw), lambda g, bi, r: (bi, r, 0, g)),
        out_shape=jax.ShapeDtypeStruct((b, rows, w, e), BF16),
        scratch_shapes=[pltpu.VMEM((NAT_HEAD_GROUP, nq, nk), F32),
                        pltpu.VMEM((NAT_HEAD_GROUP, nq, nk), BF16)],
        compiler_params=_params("parallel", "parallel", "arbitrary"),
    )(*([x4] * (1 + 2 * NAT_N_SUB)), bias)
    return out.reshape(b * s, e)


def _log_gate(g1, wa2_ref, ba_ref):
    g_hi, g_lo = _split_bf16(g1)
    w_hi = wa2_ref[0]
    w_lo = wa2_ref[1]
    pre = _dot(g_hi, w_hi) + _dot(g_lo, w_hi) + _dot(g_hi, w_lo) + ba_ref[...]
    return (jnp.minimum(pre, 0.0) - jnp.log(1.0 + jnp.exp(-jnp.abs(pre)))) / GLA_GATE_TEMP


def _gla_superchunk(q, k, v, la, state_ref, reverse):
    n, dk = q.shape
    c = GLA_CHUNK
    nb = n // c
    ii = lax.broadcasted_iota(jnp.int32, (n, n), 0)
    jj = lax.broadcasted_iota(jnp.int32, (n, n), 1)
    tri = jnp.where((jj >= ii) if reverse else (jj <= ii), 1.0, 0.0).astype(BF16)
    la_hi, la_lo = _split_bf16(la)
    bc = _dot(tri, la_hi) + _dot(tri, la_lo)
    qf = q.astype(F32) * (dk ** -0.5)
    kf = k.astype(F32)

    def brow(idx):
        return bc[idx:idx + 1]

    ka_parts = []
    mids = []
    for blk in range(nb):
        r0 = blk * c
        b_mid = brow(r0 + c // 2) if reverse else brow(r0 + c // 2 - 1)
        mids.append(b_mid)
        ka_parts.append((kf[r0:r0 + c] * jnp.exp(b_mid - bc[r0:r0 + c])).astype(BF16))
    ka = jnp.concatenate(ka_parts, axis=0)

    row_i = lax.broadcasted_iota(jnp.int32, (c, n), 0)
    col_j = lax.broadcasted_iota(jnp.int32, (c, n), 1)
    s_rows = []
    for blk in range(nb):
        r0 = blk * c
        b_blk = bc[r0:r0 + c]
        q_blk = qf[r0:r0 + c]
        qa = (q_blk * jnp.exp(b_blk - mids[blk])).astype(BF16)
        s_diag = _dot_nt(qa, ka)
        in_blk = jnp.logical_and(col_j >= r0, col_j < r0 + c)
        if reverse:
            keep = jnp.logical_and(in_blk, col_j > row_i + r0)
        else:
            keep = jnp.logical_and(in_blk, col_j <= row_i + r0)
        s_blk = jnp.where(keep, s_diag, 0.0)
        has_other = blk < nb - 1 if reverse else blk > 0
        if has_other:
            b_s = brow(r0 + c) if reverse else brow(r0 - 1)
            qo = (q_blk * jnp.exp(b_blk - b_s)).astype(BF16)
            if reverse:
                ko = (kf[r0 + c:] * jnp.exp(b_s - bc[r0 + c:])).astype(BF16)
                ko = jnp.concatenate([jnp.zeros((r0 + c, dk), BF16), ko], axis=0)
            else:
                ko = (kf[:r0] * jnp.exp(b_s - bc[:r0])).astype(BF16)
                ko = jnp.concatenate([ko, jnp.zeros((n - r0, dk), BF16)], axis=0)
            s_blk = s_blk + _dot_nt(qo, ko)
        s_rows.append(s_blk.astype(BF16))
    scores = jnp.concatenate(s_rows, axis=0)

    b_end = brow(0) if reverse else brow(n - 1)
    q_in = (qf * jnp.exp(bc)).astype(BF16)
    o = _dot(scores, v) + _dot(q_in, state_ref[...].astype(BF16))
    k_out = (kf * jnp.exp(b_end - bc)).astype(BF16)
    ones = jnp.ones((n, 128), BF16)
    decay = jnp.exp(_dot_tn(la_hi, ones) + _dot_tn(la_lo, ones))
    upd = _dot_tn(k_out, v)
    for lb in range(v.shape[1] // 128):
        sl = slice(lb * 128, (lb + 1) * 128)
        state_ref[:, sl] = decay * state_ref[:, sl] + upd[:, sl]
    return o


def _gla_fwd_kernel(q_ref, k_ref, v_ref, g1_ref, wa2_ref, ba_ref, o_ref, state_ref):
    @pl.when(pl.program_id(2) == 0)
    def _():
        state_ref[...] = jnp.zeros_like(state_ref)

    n_heads, dk, dv = state_ref.shape
    la = _log_gate(g1_ref[:, :GLA_GATE_RANK], wa2_ref, ba_ref)
    n = GLA_SUPER
    for sc in range(q_ref.shape[0] // n):
        rows = slice(sc * n, (sc + 1) * n)
        for g in range(n_heads):
            kl = slice(g * dk, (g + 1) * dk)
            vl = slice(g * dv, (g + 1) * dv)
            o = _gla_superchunk(q_ref[rows, kl], k_ref[rows, kl], v_ref[rows, vl], la[rows, kl],
                                state_ref.at[g], False)
            o_ref[rows, vl] = o.astype(o_ref.dtype)


def _gla_bwd_kernel(q_ref, k_ref, v_ref, g1_ref, wa2_ref, ba_ref, of_ref, o_ref, state_ref):
    @pl.when(pl.program_id(2) == 0)
    def _():
        state_ref[...] = jnp.zeros_like(state_ref)

    n_heads, dk, dv = state_ref.shape
    la = _log_gate(g1_ref[:, GLA_GATE_RANK:], wa2_ref, ba_ref)
    n = GLA_SUPER
    for sc in reversed(range(q_ref.shape[0] // n)):
        rows = slice(sc * n, (sc + 1) * n)
        for g in range(n_heads):
            kl = slice(g * dk, (g + 1) * dk)
            vl = slice(g * dv, (g + 1) * dv)
            o = _gla_superchunk(q_ref[rows, kl], k_ref[rows, kl], v_ref[rows, vl], la[rows, kl],
                                state_ref.at[g], True)
            o_ref[rows, vl] = (o + of_ref[rows, vl].astype(F32)).astype(o_ref.dtype)


def _gla_mixer(qkvz, g1, wa2_f, ba_f, wa2_b, ba_b, g_norm, b, s):
    hh = GLA_HEADS
    dv = g_norm.shape[0]
    e = hh * dv
    kd = (qkvz.shape[1] - 2 * e) // 2
    dk = kd // hh
    hp = GLA_HEADS_PER_STEP
    wk, wv = hp * dk, hp * dv
    assert hh % hp == 0 and kd % wk == 0 and (2 * kd) % wv == 0
    tb = min(512, s)
    assert s % tb == 0 and tb % GLA_SUPER == 0
    nt = s // tb
    x3 = qkvz.reshape(b, s, qkvz.shape[1])
    g3 = g1.reshape(b, s, 2 * GLA_GATE_RANK)
    k0 = kd // wk
    v0 = 2 * kd // wv

    def specs(tmap):
        return [pl.BlockSpec((None, tb, wk), lambda bi, h, t: (bi, tmap(t), h)),
                pl.BlockSpec((None, tb, wk), lambda bi, h, t: (bi, tmap(t), k0 + h)),
                pl.BlockSpec((None, tb, wv), lambda bi, h, t: (bi, tmap(t), v0 + h)),
                pl.BlockSpec((None, tb, 2 * GLA_GATE_RANK), lambda bi, h, t: (bi, tmap(t), 0)),
                pl.BlockSpec((2, GLA_GATE_RANK, wk), lambda bi, h, t: (0, 0, h)),
                pl.BlockSpec((1, wk), lambda bi, h, t: (0, h))]

    def gate_w(wa2):
        hi, lo = _split_bf16(wa2)
        return jnp.stack([hi, lo])

    fwd = lambda t: t
    o_f = pl.pallas_call(
        _gla_fwd_kernel,
        grid=(b, hh // hp, nt),
        in_specs=specs(fwd),
        out_specs=pl.BlockSpec((None, tb, wv), lambda bi, h, t: (bi, t, h)),
        out_shape=jax.ShapeDtypeStruct((b, s, e), BF16),
        scratch_shapes=[pltpu.VMEM((hp, dk, dv), F32)],
        compiler_params=_params("parallel", "parallel", "arbitrary"),
    )(x3, x3, x3, g3, gate_w(wa2_f), ba_f.reshape(1, kd))

    rev = lambda t: nt - 1 - t
    out = pl.pallas_call(
        _gla_bwd_kernel,
        grid=(b, hh // hp, nt),
        in_specs=specs(rev) + [pl.BlockSpec((None, tb, wv), lambda bi, h, t: (bi, rev(t), h))],
        out_specs=pl.BlockSpec((None, tb, wv), lambda bi, h, t: (bi, rev(t), h)),
        out_shape=jax.ShapeDtypeStruct((b, s, e), BF16),
        scratch_shapes=[pltpu.VMEM((hp, dk, dv), F32)],
        compiler_params=_params("parallel", "parallel", "arbitrary"),
    )(x3, x3, x3, g3, gate_w(wa2_b), ba_b.reshape(1, kd), o_f)
    return out.reshape(b * s, e)


def _fnet_layer(x, g_pre, g_post, w_in, w_out):
    b, s, d = x.shape
    x2 = x.reshape(b * s, d)
    e = w_out.shape[0]
    w_uz = jnp.concatenate([w_in[:, :e], _fnet_permute_channels(w_in[:, e:], 1)], axis=1)
    uz = _in_proj(x2, g_pre, w_uz.astype(BF16))
    a = _fnet_mixer(uz, b, s)
    w_o = _fnet_permute_channels(w_out, 0)
    return _out_proj(a, w_o.astype(BF16), g_post, x2).reshape(b, s, d)


def _nat_layer(x, g_pre, g_post, w_in, rpb, w_out):
    b, s, d = x.shape
    x2 = x.reshape(b * s, d)
    e = w_in.shape[1] // 4
    q_cols = (jnp.arange(w_in.shape[1]) < e)[None, :]
    w_scaled = jnp.where(q_cols, w_in * (NAT_HEAD_DIM ** -0.5 * LOG2_E), w_in)
    qkvz = _in_proj(x2, g_pre, w_scaled.astype(BF16))
    m = _nat_mixer(qkvz, rpb, b, s)
    return _gated_out_proj(m, qkvz, 3 * e, w_out.astype(BF16), g_post, x2).reshape(b, s, d)


def _gla_layer(x, g_pre, g_post, w_in, wa1_f, wa2_f, ba_f, wa1_b, wa2_b, ba_b, g_norm, w_out):
    b, s, d = x.shape
    x2 = x.reshape(b * s, d)
    qkvz, g1 = _in_proj(x2, g_pre, w_in.astype(BF16), jnp.concatenate([wa1_f, wa1_b], axis=1))
    m = _gla_mixer(qkvz, g1, wa2_f, ba_f, wa2_b, ba_b, g_norm, b, s)
    z_start = qkvz.shape[1] - m.shape[1]
    return _gated_out_proj(m, qkvz, z_start, w_out.astype(BF16), g_post, x2,
                           head_gain=g_norm).reshape(b, s, d)


def _trunk(x, norm_pre_g, norm_post_g, fnet_w_in, fnet_w_out, nat_w_in, nat_rpb, nat_w_out,
           gla_w_in, gla_wa1_f, gla_wa2_f, gla_ba_f, gla_wa1_b, gla_wa2_b, gla_ba_b, gla_g_norm,
           gla_w_out):
    depth = norm_pre_g.shape[0]
    for i in range(depth):
        m, j = i % 3, i // 3
        if m == 0:
            x = _fnet_layer(x, norm_pre_g[i], norm_post_g[i], fnet_w_in[j], fnet_w_out[j])
        elif m == 1:
            x = _nat_layer(x, norm_pre_g[i], norm_post_g[i], nat_w_in[j], nat_rpb[j], nat_w_out[j])
        else:
            x = _gla_layer(x, norm_pre_g[i], norm_post_g[i], gla_w_in[j], gla_wa1_f[j], gla_wa2_f[j],
                           gla_ba_f[j], gla_wa1_b[j], gla_wa2_b[j], gla_ba_b[j], gla_g_norm[j],
                           gla_w_out[j])
    return x


def kernel(x_prompt, x_sample, norm_pre_g, norm_post_g, fnet_w_in, fnet_w_out, nat_w_in, nat_rpb,
           nat_w_out, gla_w_in, gla_wa1_f, gla_wa2_f, gla_ba_f, gla_wa1_b, gla_wa2_b, gla_ba_b,
           gla_g_norm, gla_w_out):
    params = (norm_pre_g, norm_post_g, fnet_w_in, fnet_w_out, nat_w_in, nat_rpb, nat_w_out,
              gla_w_in, gla_wa1_f, gla_wa2_f, gla_ba_f, gla_wa1_b, gla_wa2_b, gla_ba_b, gla_g_norm,
              gla_w_out)
    return (_trunk(x_prompt, *params), _trunk(x_sample, *params))
```

```python
import functools
import math

import numpy as np
import jax
import jax.numpy as jnp
from jax import lax
from jax.experimental import pallas as pl
from jax.experimental.pallas import tpu as pltpu

F32 = jnp.float32
BF16 = jnp.bfloat16

RMS_EPS = 1e-6
GRID_W = 64
FNET_GROUP_W = 512
DFT_N2 = 128
NAT_HEAD_DIM = 128
NAT_WIN_H = 8
NAT_WIN_W = 16
NAT_Q_ROWS = 8
NAT_KEY_SUB = 4
NAT_N_SUB = 4
NAT_HEAD_GROUP = 4
GLA_HEADS = 4
GLA_GATE_RANK = 16
GLA_GATE_TEMP = 16.0
GLA_CHUNK = 64
GLA_SUPER = 256
GLA_HEADS_PER_STEP = 2
NEG_MASK = -1e30
LOG2_E = 1.4426950408889634

V7X_VMEM_LIMIT_BYTES = 58 * 1024 * 1024


def _params(*sem):
    return pltpu.CompilerParams(dimension_semantics=sem, vmem_limit_bytes=V7X_VMEM_LIMIT_BYTES)


def _silu(z):
    return z / (1.0 + jnp.exp(-z))


def _dot(a, b):
    return jnp.dot(a, b, preferred_element_type=F32)


def _dot_nt(a, b):
    return lax.dot_general(a, b, (((1,), (1,)), ((), ())), preferred_element_type=F32)


def _dot_tn(a, b):
    return lax.dot_general(a, b, (((0,), (0,)), ((), ())), preferred_element_type=F32)


def _split_bf16(a):
    hi = a.astype(BF16)
    lo = (a - hi.astype(F32)).astype(BF16)
    return hi, lo


def _in_proj_kernel(*refs, has_aux, row_chunk):
    if has_aux:
        x_ref, g_ref, w_ref, wa_ref, o_ref, aux_ref, h_ref = refs
    else:
        x_ref, g_ref, w_ref, o_ref, h_ref = refs

    @pl.when(pl.program_id(1) == 0)
    def _():
        def body(c, carry):
            r = pl.multiple_of(c * row_chunk, row_chunk)
            x = x_ref[pl.ds(r, row_chunk), :]
            ms = jnp.mean(x * x, axis=-1, keepdims=True)
            hn = x * lax.rsqrt(ms + RMS_EPS) * g_ref[...]
            h_ref[pl.ds(r, row_chunk), :] = hn.astype(BF16)
            if has_aux:
                h_hi, h_lo = _split_bf16(hn)
                w_hi = wa_ref[0]
                w_lo = wa_ref[1]
                aux_ref[pl.ds(r, row_chunk), :] = _dot(h_hi, w_hi) + _dot(h_lo, w_hi) + _dot(h_hi, w_lo)
            return carry
        lax.fori_loop(0, x_ref.shape[0] // row_chunk, body, 0)

    o_ref[...] = _dot(h_ref[...], w_ref[...]).astype(o_ref.dtype)


def _in_proj(x2d, g, w_bf16, w_aux=None):
    t, d = x2d.shape
    n = w_bf16.shape[1]
    tm = min(1024, t)
    tn = min(2048, n)
    assert t % tm == 0 and n % tn == 0 and tm % 128 == 0
    has_aux = w_aux is not None
    in_specs = [
        pl.BlockSpec((tm, d), lambda i, j: (i, 0)),
        pl.BlockSpec((1, d), lambda i, j: (0, 0)),
        pl.BlockSpec((d, tn), lambda i, j: (0, j)),
    ]
    args = [x2d, g.reshape(1, d), w_bf16]
    out_shape = [jax.ShapeDtypeStruct((t, n), BF16)]
    out_specs = [pl.BlockSpec((tm, tn), lambda i, j: (i, j))]
    if has_aux:
        na = w_aux.shape[1]
        hi, lo = _split_bf16(w_aux)
        in_specs.append(pl.BlockSpec((2, d, na), lambda i, j: (0, 0, 0)))
        args.append(jnp.stack([hi, lo]))
        out_shape.append(jax.ShapeDtypeStruct((t, na), F32))
        out_specs.append(pl.BlockSpec((tm, na), lambda i, j: (i, 0)))
    res = pl.pallas_call(
        functools.partial(_in_proj_kernel, has_aux=has_aux, row_chunk=128),
        grid=(t // tm, n // tn),
        in_specs=in_specs,
        out_specs=out_specs,
        out_shape=out_shape,
        scratch_shapes=[pltpu.VMEM((tm, d), BF16)],
        compiler_params=_params("parallel", "arbitrary"),
    )(*args)
    return res if has_aux else res[0]


def _out_proj_kernel(a_ref, w_ref, g_ref, x_ref, o_ref):
    t = _dot(a_ref[...], w_ref[...])
    ms = jnp.mean(t * t, axis=-1, keepdims=True)
    o_ref[...] = x_ref[...] + t * lax.rsqrt(ms + RMS_EPS) * g_ref[...]


def _out_proj(a2d, w_bf16, g, x2d):
    t, e = a2d.shape
    d = w_bf16.shape[1]
    tm = min(512, t)
    assert t % tm == 0
    return pl.pallas_call(
        _out_proj_kernel,
        grid=(t // tm,),
        in_specs=[pl.BlockSpec((tm, e), lambda i: (i, 0)),
                  pl.BlockSpec((e, d), lambda i: (0, 0), pipeline_mode=pl.Buffered(1)),
                  pl.BlockSpec((1, d), lambda i: (0, 0)),
                  pl.BlockSpec((tm, d), lambda i: (i, 0))],
        out_specs=pl.BlockSpec((tm, d), lambda i: (i, 0)),
        out_shape=jax.ShapeDtypeStruct((t, d), F32),
        compiler_params=_params("parallel"),
    )(a2d, w_bf16, g.reshape(1, d), x2d)


def _gated_out_proj_kernel(*refs, n_z, head_norm_width):
    m_ref = refs[0]
    z_refs = refs[1:1 + n_z]
    if head_norm_width:
        gn_ref, w_ref, g_ref, x_ref, o_ref = refs[1 + n_z:]
    else:
        w_ref, g_ref, x_ref, o_ref = refs[1 + n_z:]
    e = m_ref.shape[1]
    zw = e // n_z
    cw = head_norm_width if head_norm_width else min(zw, 1024)
    t = None
    for c in range(e // cw):
        sl = slice(c * cw, (c + 1) * cw)
        m = m_ref[:, sl].astype(F32)
        if head_norm_width:
            m = m * lax.rsqrt(jnp.mean(m * m, axis=-1, keepdims=True) + RMS_EPS) * gn_ref[...]
        zi, zo = divmod(c * cw, zw)
        z = z_refs[zi][:, zo:zo + cw].astype(F32)
        part = _dot((m * _silu(z)).astype(BF16), w_ref[sl, :])
        t = part if t is None else t + part
    ms = jnp.mean(t * t, axis=-1, keepdims=True)
    o_ref[...] = x_ref[...] + t * lax.rsqrt(ms + RMS_EPS) * g_ref[...]


def _gated_out_proj(m2d, proj, z_start, w_bf16, g, x2d, head_gain=None):
    t, e = m2d.shape
    d = w_bf16.shape[1]
    tm = min(256, t)
    n_z = 2
    zw = e // n_z
    assert t % tm == 0 and z_start % zw == 0
    zb = z_start // zw
    in_specs = [pl.BlockSpec((tm, e), lambda i: (i, 0))]
    in_specs += [pl.BlockSpec((tm, zw), lambda i, j=j: (i, zb + j)) for j in range(n_z)]
    args = [m2d] + [proj] * n_z
    hw = 0
    if head_gain is not None:
        hw = head_gain.shape[0]
        assert zw % hw == 0
        in_specs.append(pl.BlockSpec((1, hw), lambda i: (0, 0)))
        args.append(head_gain.reshape(1, hw).astype(F32))
    in_specs += [pl.BlockSpec((e, d), lambda i: (0, 0), pipeline_mode=pl.Buffered(1)),
                 pl.BlockSpec((1, d), lambda i: (0, 0)),
                 pl.BlockSpec((tm, d), lambda i: (i, 0))]
    args += [w_bf16, g.reshape(1, d), x2d]
    return pl.pallas_call(
        functools.partial(_gated_out_proj_kernel, n_z=n_z, head_norm_width=hw),
        grid=(t // tm,),
        in_specs=in_specs,
        out_specs=pl.BlockSpec((tm, d), lambda i: (i, 0)),
        out_shape=jax.ShapeDtypeStruct((t, d), F32),
        compiler_params=_params("parallel"),
    )(*args)


def _dft_tables(n1, gb, nk):
    n2 = DFT_N2
    n = n1 * n2
    ang_a = 2.0 * np.pi * np.outer(np.arange(nk), np.arange(n1)) / n1
    eye = np.eye(gb)
    fa_re = np.kron(eye, np.cos(ang_a))
    fa_im = np.kron(eye, -np.sin(ang_a))
    fa = np.concatenate([fa_re, fa_im], axis=0) / math.sqrt(n1)

    kk1 = jnp.arange(nk, dtype=jnp.int32)[:, None, None]
    kk2 = jnp.arange(n2, dtype=jnp.int32)[None, :, None]
    nn2 = jnp.arange(n2, dtype=jnp.int32)[None, None, :]
    m = (nn2 * (kk1 + n1 * kk2)) % n
    ang = m.astype(F32) * (2.0 * math.pi / n)
    c = jnp.cos(ang) / math.sqrt(n2)
    s = jnp.sin(ang) / math.sqrt(n2)
    mtab = jnp.concatenate([jnp.concatenate([c, s], axis=2), jnp.concatenate([-s, c], axis=2)], axis=1)

    cw = FNET_GROUP_W
    ang_c = 2.0 * np.pi * (np.outer(np.arange(cw), np.arange(cw)) % cw) / cw
    cc = np.cos(ang_c)[:, :cw // 2] / math.sqrt(cw)
    sc = np.sin(ang_c)[:, :cw // 2] / math.sqrt(cw)
    return (jnp.asarray(fa, F32).astype(BF16), mtab.astype(BF16),
            jnp.asarray(cc, F32).astype(BF16), jnp.asarray(sc, F32).astype(BF16))


def _dft_a_kernel(f_ref, u_ref, o_ref):
    o_ref[...] = _dot(f_ref[...], u_ref[...]).astype(o_ref.dtype)


def _dft_c_kernel(a_ref, m_ref, cc_ref, sc_ref, zp_ref, zm_ref, op_ref, om_ref, zr_ref, zi_ref,
                  *, bb, cb):
    n2 = DFT_N2
    for bi in range(bb):
        zz = _dot(m_ref[0], a_ref[0, bi])
        zr_ref[bi * n2:(bi + 1) * n2, :] = zz[:n2].astype(BF16)
        zi_ref[bi * n2:(bi + 1) * n2, :] = zz[n2:].astype(BF16)
    cw = FNET_GROUP_W
    hw = cw // 2
    rows = bb * n2
    ii = lax.broadcasted_iota(jnp.int32, (n2, n2), 0)
    jj = lax.broadcasted_iota(jnp.int32, (n2, n2), 1)
    exchange = jnp.where(ii + jj == n2 - 1, 1.0, 0.0).astype(BF16)
    lane = lax.broadcasted_iota(jnp.int32, (1, cw), 1)
    alt_sign = jnp.where(lane % 2 == 0, 1.0, -1.0) * (cw ** -0.5)
    first = lax.broadcasted_iota(jnp.int32, (1, hw), 1) == 0
    for gi in range(cb // cw):
        sl = slice(gi * cw, (gi + 1) * cw)
        zr = zr_ref[:, sl]
        p = _dot(zr, cc_ref[...])
        q = _dot(zi_ref[:, sl], sc_ref[...])
        y_mid = jnp.sum(zr.astype(F32) * alt_sign, axis=-1, keepdims=True)
        ev = p + q
        od = jnp.where(first, y_mid, p - q)
        y_p = jnp.concatenate([ev, od], axis=-1)
        y_m = jnp.concatenate([jnp.where(first, ev, od), jnp.where(first, od, ev)], axis=-1)
        y_m = y_m.astype(BF16)
        y_m = jnp.concatenate([_dot(exchange, y_m[bi * n2:(bi + 1) * n2]) for bi in range(bb)], axis=0)
        z_p = zp_ref[:, :, :, sl].reshape(rows, cw).astype(F32)
        z_m = zm_ref[:, :, :, sl].reshape(rows, cw).astype(F32)
        op_ref[:, :, :, sl] = (y_p * _silu(z_p)).astype(op_ref.dtype).reshape(1, bb, n2, cw)
        om_ref[:, :, :, sl] = (y_m * _silu(z_m)).astype(om_ref.dtype).reshape(1, bb, n2, cw)


def _fnet_permute_channels(w, axis):
    cw = FNET_GROUP_W
    within = np.concatenate([np.arange(cw // 2 + 1), np.arange(cw - 1, cw // 2, -1)])
    perm = (np.arange(0, w.shape[axis], cw)[:, None] + within[None, :]).reshape(-1)
    return jnp.take(w, jnp.asarray(perm), axis=axis)


def _fnet_mixer(uz, b, s):
    e = uz.shape[1] // 2
    n2 = DFT_N2
    assert s % n2 == 0
    n1 = s // n2
    gb = max(1, min(b, 256 // n1))
    while b % gb:
        gb -= 1
    rg = gb * n1
    ng = b // gb
    half = n1 // 2
    assert n1 % 2 == 0 and half >= 2
    nk = half + 1
    if (gb * nk) % 8:
        nk = min(n1, -(-nk // 8) * 8)
    fa, mtab, cc, sc = _dft_tables(n1, gb, nk)

    uz4 = uz.reshape(b, n1, n2, 2 * e)
    u_t = jnp.transpose(uz4[..., :e], (2, 0, 1, 3)).reshape(n2, ng, rg, e)
    ca = min(4096, e)
    ra = 2 * gb * nk
    a_nat = pl.pallas_call(
        _dft_a_kernel,
        grid=(n2, ng, e // ca),
        in_specs=[pl.BlockSpec((ra, rg), lambda n, g, c: (0, 0)),
                  pl.BlockSpec((None, None, rg, ca), lambda n, g, c: (n, g, 0, c))],
        out_specs=pl.BlockSpec((None, None, ra, ca), lambda n, g, c: (n, g, 0, c)),
        out_shape=jax.ShapeDtypeStruct((n2, ng, ra, e), BF16),
        compiler_params=_params("parallel", "parallel", "parallel"),
    )(fa, u_t)

    a3 = jnp.transpose(a_nat.reshape(n2, ng, 2, gb, nk, e), (4, 1, 3, 2, 0, 5)).reshape(nk, b, 2 * n2, e)
    z_t = jnp.transpose(uz.reshape(b, n2, n1, 2 * e)[..., e:], (2, 0, 1, 3))

    bb = min(b, 8)
    while b % bb:
        bb -= 1
    cb = min(1024, e)
    blk = lambda i, j, c: (i, j, 0, c)
    mirror = lambda i, j, c: ((n1 - i) % n1, j, 0, c)
    a_p, a_m = pl.pallas_call(
        functools.partial(_dft_c_kernel, bb=bb, cb=cb),
        grid=(nk, b // bb, e // cb),
        in_specs=[pl.BlockSpec((1, bb, 2 * n2, cb), blk),
                  pl.BlockSpec((1, 2 * n2, 2 * n2), lambda i, j, c: (i, 0, 0)),
                  pl.BlockSpec((FNET_GROUP_W, FNET_GROUP_W // 2), lambda i, j, c: (0, 0)),
                  pl.BlockSpec((FNET_GROUP_W, FNET_GROUP_W // 2), lambda i, j, c: (0, 0)),
                  pl.BlockSpec((1, bb, n2, cb), blk),
                  pl.BlockSpec((1, bb, n2, cb), mirror)],
        out_specs=[pl.BlockSpec((1, bb, n2, cb), blk), pl.BlockSpec((1, bb, n2, cb), mirror)],
        out_shape=[jax.ShapeDtypeStruct((nk, b, n2, e), BF16),
                   jax.ShapeDtypeStruct((n1, b, n2, e), BF16)],
        scratch_shapes=[pltpu.VMEM((bb * n2, cb), BF16), pltpu.VMEM((bb * n2, cb), BF16)],
        compiler_params=_params("parallel", "parallel", "arbitrary"),
    )(a3, mtab, cc, sc, z_t, z_t)

    lo = jnp.transpose(a_p[:half + 1], (1, 2, 0, 3))
    hi = jnp.transpose(a_m[half + 1:], (1, 2, 0, 3))
    return jnp.concatenate([lo, hi], axis=2).reshape(b * s, e)


def _nat_bias_pairs(rpb):
    qc = np.arange(GRID_W)[:, None]
    kc = np.arange(GRID_W)[None, :]
    ws = np.clip(qc - NAT_WIN_W // 2, 0, GRID_W - NAT_WIN_W)
    valid = (kc >= ws) & (kc < ws + NAT_WIN_W)
    rel = np.clip(kc - qc, -(NAT_WIN_W - 1), NAT_WIN_W - 1) + NAT_WIN_W - 1
    colb = jnp.where(jnp.asarray(valid)[None, None], rpb[:, :, jnp.asarray(rel)].astype(F32), NEG_MASK)
    return jnp.concatenate([colb[:, :-1], colb[:, 1:]], axis=-1) * LOG2_E


def _nat_kernel(*refs, n_rb, heads):
    q_ref = refs[0]
    k_refs = refs[1:1 + NAT_N_SUB]
    v_refs = refs[1 + NAT_N_SUB:1 + 2 * NAT_N_SUB]
    bias_ref, o_ref, s_ref, p_ref = refs[1 + 2 * NAT_N_SUB:]
    hd = NAT_HEAD_DIM
    w = GRID_W
    nq = NAT_Q_ROWS * w
    nks = NAT_KEY_SUB * w
    lane = lax.broadcasted_iota(jnp.int32, (w, 2 * w), 1)

    rows_half = NAT_Q_ROWS // 2
    hq = rows_half * w
    subs_half = NAT_N_SUB - 1
    pairs_sub = NAT_KEY_SUB // 2

    def run(window_starts):
        def scores(h, slot):
            off = pl.multiple_of(h * hd, hd)
            ks = [k_refs[j][:, :, pl.ds(off, hd)].reshape(nks, hd) for j in range(NAT_N_SUB)]
            for a in range(2):
                q = q_ref[a * rows_half:(a + 1) * rows_half, :, pl.ds(off, hd)].reshape(hq, hd)
                for j in range(a, a + subs_half):
                    s_ref[slot, a * hq:(a + 1) * hq, j * nks:(j + 1) * nks] = _dot_nt(q, ks[j])

        def softmax(h, slot):
            for i in range(NAT_Q_ROWS):
                lo = window_starts[i]
                kp_lo, kp_hi = lo // 2, (lo + NAT_WIN_H - 1) // 2
                a = i // rows_half
                assert a * pairs_sub <= kp_lo and kp_hi < (a + subs_half) * pairs_sub
                rows = slice(i * w, (i + 1) * w)
                tiles = []
                for kp in range(kp_lo, kp_hi + 1):
                    t = s_ref[slot, rows, kp * 2 * w:(kp + 1) * 2 * w] + bias_ref[h, 2 * kp - i + 3]
                    if 2 * kp < lo:
                        t = jnp.where(lane >= w, t, NEG_MASK)
                    if 2 * kp + 1 >= lo + NAT_WIN_H:
                        t = jnp.where(lane < w, t, NEG_MASK)
                    tiles.append(t)
                m = tiles[0]
                for t in tiles[1:]:
                    m = jnp.maximum(m, t)
                m = jnp.max(m, axis=-1, keepdims=True)
                es = [jnp.exp2(t - m) for t in tiles]
                tot = es[0]
                for ee in es[1:]:
                    tot = tot + ee
                inv = 1.0 / jnp.sum(tot, axis=-1, keepdims=True)
                for kp in range(a * pairs_sub, (a + subs_half) * pairs_sub):
                    cols = slice(kp * 2 * w, (kp + 1) * 2 * w)
                    if kp_lo <= kp <= kp_hi:
                        p_ref[slot, rows, cols] = (es[kp - kp_lo] * inv).astype(BF16)
                    else:
                        p_ref[slot, rows, cols] = jnp.zeros((w, 2 * w), BF16)

        def weighted_sum(h, slot):
            off = pl.multiple_of(h * hd, hd)
            vs = [v_refs[j][:, :, pl.ds(off, hd)].reshape(nks, hd) for j in range(NAT_N_SUB)]
            for a in range(2):
                acc = None
                for j in range(a, a + subs_half):
                    part = _dot(p_ref[slot, a * hq:(a + 1) * hq, j * nks:(j + 1) * nks], vs[j])
                    acc = part if acc is None else acc + part
                qrows = slice(a * rows_half, (a + 1) * rows_half)
                o_ref[qrows, :, pl.ds(off, hd)] = acc.astype(o_ref.dtype).reshape(rows_half, w, hd)

        def group_body(t, carry):
            hs = [NAT_HEAD_GROUP * t + g for g in range(NAT_HEAD_GROUP)]
            for g, h in enumerate(hs):
                scores(h, g)
            for g, h in enumerate(hs):
                softmax(h, g)
            for g, h in enumerate(hs):
                weighted_sum(h, g)
            return carry
        lax.fori_loop(0, heads // NAT_HEAD_GROUP, group_body, 0)

    rb = pl.program_id(2)
    half = NAT_WIN_H // 2
    interior = list(range(NAT_Q_ROWS))
    top = [max(i, half) for i in range(NAT_Q_ROWS)]
    bottom = [min(i, half) for i in range(NAT_Q_ROWS)]

    @pl.when(rb == 0)
    def _():
        run(top)

    @pl.when(rb == n_rb - 1)
    def _():
        run(bottom)

    @pl.when(jnp.logical_and(rb > 0, rb < n_rb - 1))
    def _():
        run(interior)


def _nat_mixer(qkvz, rpb, b, s):
    e = qkvz.shape[1] // 4
    w = GRID_W
    rows = s // w
    assert s % w == 0 and rows % NAT_Q_ROWS == 0 and rows >= 2 * NAT_Q_ROWS
    n_rb = rows // NAT_Q_ROWS
    hb = 8
    lw = hb * NAT_HEAD_DIM
    nhg = e // lw
    x4 = qkvz.reshape(b, rows, w, 4 * e)
    bias = _nat_bias_pairs(rpb)
    n_kblk = rows // NAT_KEY_SUB

    def kv_spec(j, sec):
        def imap(g, bi, r):
            blk = jnp.clip(2 * r - 1 + j, 0, n_kblk - 1)
            return (bi, blk, 0, sec * nhg + g)
        return pl.BlockSpec((None, NAT_KEY_SUB, w, lw), imap)

    in_specs = ([pl.BlockSpec((None, NAT_Q_ROWS, w, lw), lambda g, bi, r: (bi, r, 0, g))]
                + [kv_spec(j, 1) for j in range(NAT_N_SUB)]
                + [kv_spec(j, 2) for j in range(NAT_N_SUB)]
                + [pl.BlockSpec((hb, 2 * NAT_WIN_H - 2, w, 2 * w), lambda g, bi, r: (g, 0, 0, 0))])
    nq = NAT_Q_ROWS * w
    nk = NAT_N_SUB * NAT_KEY_SUB * w
    out = pl.pallas_call(
        functools.partial(_nat_kernel, n_rb=n_rb, heads=hb),
        grid=(nhg, b, n_rb),
        in_specs=in_specs,
        out_specs=pl.BlockSpec((None, NAT_Q_ROWS, w, lw), lambda g, bi, r: (bi, r, 0, g)),
        out_shape=jax.ShapeDtypeStruct((b, rows, w, e), BF16),
        scratch_shapes=[pltpu.VMEM((NAT_HEAD_GROUP, nq, nk), F32),
                        pltpu.VMEM((NAT_HEAD_GROUP, nq, nk), BF16)],
        compiler_params=_params("parallel", "parallel", "arbitrary"),
    )(*([x4] * (1 + 2 * NAT_N_SUB)), bias)
    return out.reshape(b * s, e)


def _log_gate(g1, wa2_ref, ba_ref):
    g_hi, g_lo = _split_bf16(g1)
    w_hi = wa2_ref[0]
    w_lo = wa2_ref[1]
    pre = _dot(g_hi, w_hi) + _dot(g_lo, w_hi) + _dot(g_hi, w_lo) + ba_ref[...]
    return (jnp.minimum(pre, 0.0) - jnp.log(1.0 + jnp.exp(-jnp.abs(pre)))) / GLA_GATE_TEMP


def _gla_superchunk(q, k, v, la, state_ref, reverse):
    n, dk = q.shape
    c = GLA_CHUNK
    nb = n // c
    ii = lax.broadcasted_iota(jnp.int32, (n, n), 0)
    jj = lax.broadcasted_iota(jnp.int32, (n, n), 1)
    tri = jnp.where((jj >= ii) if reverse else (jj <= ii), 1.0, 0.0).astype(BF16)
    la_hi, la_lo = _split_bf16(la)
    bc = _dot(tri, la_hi) + _dot(tri, la_lo)
    qf = q.astype(F32) * (dk ** -0.5)
    kf = k.astype(F32)

    def brow(idx):
        return bc[idx:idx + 1]

    ka_parts = []
    mids = []
    for blk in range(nb):
        r0 = blk * c
        b_mid = brow(r0 + c // 2) if reverse else brow(r0 + c // 2 - 1)
        mids.append(b_mid)
        ka_parts.append((kf[r0:r0 + c] * jnp.exp(b_mid - bc[r0:r0 + c])).astype(BF16))
    ka = jnp.concatenate(ka_parts, axis=0)

    row_i = lax.broadcasted_iota(jnp.int32, (c, n), 0)
    col_j = lax.broadcasted_iota(jnp.int32, (c, n), 1)
    s_rows = []
    for blk in range(nb):
        r0 = blk * c
        b_blk = bc[r0:r0 + c]
        q_blk = qf[r0:r0 + c]
        qa = (q_blk * jnp.exp(b_blk - mids[blk])).astype(BF16)
        s_diag = _dot_nt(qa, ka)
        in_blk = jnp.logical_and(col_j >= r0, col_j < r0 + c)
        if reverse:
            keep = jnp.logical_and(in_blk, col_j > row_i + r0)
        else:
            keep = jnp.logical_and(in_blk, col_j <= row_i + r0)
        s_blk = jnp.where(keep, s_diag, 0.0)
        has_other = blk < nb - 1 if reverse else blk > 0
        if has_other:
            b_s = brow(r0 + c) if reverse else brow(r0 - 1)
            qo = (q_blk * jnp.exp(b_blk - b_s)).astype(BF16)
            if reverse:
                ko = (kf[r0 + c:] * jnp.exp(b_s - bc[r0 + c:])).astype(BF16)
                ko = jnp.concatenate([jnp.zeros((r0 + c, dk), BF16), ko], axis=0)
            else:
                ko = (kf[:r0] * jnp.exp(b_s - bc[:r0])).astype(BF16)
                ko = jnp.concatenate([ko, jnp.zeros((n - r0, dk), BF16)], axis=0)
            s_blk = s_blk + _dot_nt(qo, ko)
        s_rows.append(s_blk.astype(BF16))
    scores = jnp.concatenate(s_rows, axis=0)

    b_end = brow(0) if reverse else brow(n - 1)
    q_in = (qf * jnp.exp(bc)).astype(BF16)
    o = _dot(scores, v) + _dot(q_in, state_ref[...].astype(BF16))
    k_out = (kf * jnp.exp(b_end - bc)).astype(BF16)
    ones = jnp.ones((n, 128), BF16)
    decay = jnp.exp(_dot_tn(la_hi, ones) + _dot_tn(la_lo, ones))
    upd = _dot_tn(k_out, v)
    for lb in range(v.shape[1] // 128):
        sl = slice(lb * 128, (lb + 1) * 128)
        state_ref[:, sl] = decay * state_ref[:, sl] + upd[:, sl]
    return o


def _gla_fwd_kernel(q_ref, k_ref, v_ref, g1_ref, wa2_ref, ba_ref, o_ref, state_ref):
    @pl.when(pl.program_id(2) == 0)
    def _():
        state_ref[...] = jnp.zeros_like(state_ref)

    n_heads, dk, dv = state_ref.shape
    la = _log_gate(g1_ref[:, :GLA_GATE_RANK], wa2_ref, ba_ref)
    n = GLA_SUPER
    for sc in range(q_ref.shape[0] // n):
        rows = slice(sc * n, (sc + 1) * n)
        for g in range(n_heads):
            kl = slice(g * dk, (g + 1) * dk)
            vl = slice(g * dv, (g + 1) * dv)
            o = _gla_superchunk(q_ref[rows, kl], k_ref[rows, kl], v_ref[rows, vl], la[rows, kl],
                                state_ref.at[g], False)
            o_ref[rows, vl] = o.astype(o_ref.dtype)


def _gla_bwd_kernel(q_ref, k_ref, v_ref, g1_ref, wa2_ref, ba_ref, of_ref, o_ref, state_ref):
    @pl.when(pl.program_id(2) == 0)
    def _():
        state_ref[...] = jnp.zeros_like(state_ref)

    n_heads, dk, dv = state_ref.shape
    la = _log_gate(g1_ref[:, GLA_GATE_RANK:], wa2_ref, ba_ref)
    n = GLA_SUPER
    for sc in reversed(range(q_ref.shape[0] // n)):
        rows = slice(sc * n, (sc + 1) * n)
        for g in range(n_heads):
            kl = slice(g * dk, (g + 1) * dk)
            vl = slice(g * dv, (g + 1) * dv)
            o = _gla_superchunk(q_ref[rows, kl], k_ref[rows, kl], v_ref[rows, vl], la[rows, kl],
                                state_ref.at[g], True)
            o_ref[rows, vl] = (o + of_ref[rows, vl].astype(F32)).astype(o_ref.dtype)


def _gla_mixer(qkvz, g1, wa2_f, ba_f, wa2_b, ba_b, g_norm, b, s):
    hh = GLA_HEADS
    dv = g_norm.shape[0]
    e = hh * dv
    kd = (qkvz.shape[1] - 2 * e) // 2
    dk = kd // hh
    hp = GLA_HEADS_PER_STEP
    wk, wv = hp * dk, hp * dv
    assert hh % hp == 0 and kd % wk == 0 and (2 * kd) % wv == 0
    tb = min(512, s)
    assert s % tb == 0 and tb % GLA_SUPER == 0
    nt = s // tb
    x3 = qkvz.reshape(b, s, qkvz.shape[1])
    g3 = g1.reshape(b, s, 2 * GLA_GATE_RANK)
    k0 = kd // wk
    v0 = 2 * kd // wv

    def specs(tmap):
        return [pl.BlockSpec((None, tb, wk), lambda bi, h, t: (bi, tmap(t), h)),
                pl.BlockSpec((None, tb, wk), lambda bi, h, t: (bi, tmap(t), k0 + h)),
                pl.BlockSpec((None, tb, wv), lambda bi, h, t: (bi, tmap(t), v0 + h)),
                pl.BlockSpec((None, tb, 2 * GLA_GATE_RANK), lambda bi, h, t: (bi, tmap(t), 0)),
                pl.BlockSpec((2, GLA_GATE_RANK, wk), lambda bi, h, t: (0, 0, h)),
                pl.BlockSpec((1, wk), lambda bi, h, t: (0, h))]

    def gate_w(wa2):
        hi, lo = _split_bf16(wa2)
        return jnp.stack([hi, lo])

    fwd = lambda t: t
    o_f = pl.pallas_call(
        _gla_fwd_kernel,
        grid=(b, hh // hp, nt),
        in_specs=specs(fwd),
        out_specs=pl.BlockSpec((None, tb, wv), lambda bi, h, t: (bi, t, h)),
        out_shape=jax.ShapeDtypeStruct((b, s, e), BF16),
        scratch_shapes=[pltpu.VMEM((hp, dk, dv), F32)],
        compiler_params=_params("parallel", "parallel", "arbitrary"),
    )(x3, x3, x3, g3, gate_w(wa2_f), ba_f.reshape(1, kd))

    rev = lambda t: nt - 1 - t
    out = pl.pallas_call(
        _gla_bwd_kernel,
        grid=(b, hh // hp, nt),
        in_specs=specs(rev) + [pl.BlockSpec((None, tb, wv), lambda bi, h, t: (bi, rev(t), h))],
        out_specs=pl.BlockSpec((None, tb, wv), lambda bi, h, t: (bi, rev(t), h)),
        out_shape=jax.ShapeDtypeStruct((b, s, e), BF16),
        scratch_shapes=[pltpu.VMEM((hp, dk, dv), F32)],
        compiler_params=_params("parallel", "parallel", "arbitrary"),
    )(x3, x3, x3, g3, gate_w(wa2_b), ba_b.reshape(1, kd), o_f)
    return out.reshape(b * s, e)


def _fnet_layer(x, g_pre, g_post, w_in, w_out):
    b, s, d = x.shape
    x2 = x.reshape(b * s, d)
    e = w_out.shape[0]
    w_uz = jnp.concatenate([w_in[:, :e], _fnet_permute_channels(w_in[:, e:], 1)], axis=1)
    uz = _in_proj(x2, g_pre, w_uz.astype(BF16))
    a = _fnet_mixer(uz, b, s)
    w_o = _fnet_permute_channels(w_out, 0)
    return _out_proj(a, w_o.astype(BF16), g_post, x2).reshape(b, s, d)


def _nat_layer(x, g_pre, g_post, w_in, rpb, w_out):
    b, s, d = x.shape
    x2 = x.reshape(b * s, d)
    e = w_in.shape[1] // 4
    q_cols = (jnp.arange(w_in.shape[1]) < e)[None, :]
    w_scaled = jnp.where(q_cols, w_in * (NAT_HEAD_DIM ** -0.5 * LOG2_E), w_in)
    qkvz = _in_proj(x2, g_pre, w_scaled.astype(BF16))
    m = _nat_mixer(qkvz, rpb, b, s)
    return _gated_out_proj(m, qkvz, 3 * e, w_out.astype(BF16), g_post, x2).reshape(b, s, d)


def _gla_layer(x, g_pre, g_post, w_in, wa1_f, wa2_f, ba_f, wa1_b, wa2_b, ba_b, g_norm, w_out):
    b, s, d = x.shape
    x2 = x.reshape(b * s, d)
    qkvz, g1 = _in_proj(x2, g_pre, w_in.astype(BF16), jnp.concatenate([wa1_f, wa1_b], axis=1))
    m = _gla_mixer(qkvz, g1, wa2_f, ba_f, wa2_b, ba_b, g_norm, b, s)
    z_start = qkvz.shape[1] - m.shape[1]
    return _gated_out_proj(m, qkvz, z_start, w_out.astype(BF16), g_post, x2,
                           head_gain=g_norm).reshape(b, s, d)


def _trunk(x, norm_pre_g, norm_post_g, fnet_w_in, fnet_w_out, nat_w_in, nat_rpb, nat_w_out,
           gla_w_in, gla_wa1_f, gla_wa2_f, gla_ba_f, gla_wa1_b, gla_wa2_b, gla_ba_b, gla_g_norm,
           gla_w_out):
    depth = norm_pre_g.shape[0]
    for i in range(depth):
        m, j = i % 3, i // 3
        if m == 0:
            x = _fnet_layer(x, norm_pre_g[i], norm_post_g[i], fnet_w_in[j], fnet_w_out[j])
        elif m == 1:
            x = _nat_layer(x, norm_pre_g[i], norm_post_g[i], nat_w_in[j], nat_rpb[j], nat_w_out[j])
        else:
            x = _gla_layer(x, norm_pre_g[i], norm_post_g[i], gla_w_in[j], gla_wa1_f[j], gla_wa2_f[j],
                           gla_ba_f[j], gla_wa1_b[j], gla_wa2_b[j], gla_ba_b[j], gla_g_norm[j],
                           gla_w_out[j])
    return x


def kernel(x_prompt, x_sample, norm_pre_g, norm_post_g, fnet_w_in, fnet_w_out, nat_w_in, nat_rpb,
           nat_w_out, gla_w_in, gla_wa1_f, gla_wa2_f, gla_ba_f, gla_wa1_b, gla_wa2_b, gla_ba_b,
           gla_g_norm, gla_w_out):
    params = (norm_pre_g, norm_post_g, fnet_w_in, fnet_w_out, nat_w_in, nat_rpb, nat_w_out,
              gla_w_in, gla_wa1_f, gla_wa2_f, gla_ba_f, gla_wa1_b, gla_wa2_b, gla_ba_b, gla_g_norm,
              gla_w_out)
    return (_trunk(x_prompt, *params), _trunk(x_sample, *params))
```

```python
import functools
import math

import numpy as np
import jax
import jax.numpy as jnp
from jax import lax
from jax.experimental import pallas as pl
from jax.experimental.pallas import tpu as pltpu

F32 = jnp.float32
BF16 = jnp.bfloat16

RMS_EPS = 1e-6
GRID_W = 64
FNET_GROUP_W = 512
DFT_N2 = 128
NAT_HEAD_DIM = 128
NAT_WIN_H = 8
NAT_WIN_W = 16
NAT_Q_ROWS = 8
NAT_KEY_SUB = 4
NAT_N_SUB = 4
NAT_HEAD_GROUP = 8
GLA_HEADS = 4
GLA_GATE_RANK = 16
GLA_GATE_TEMP = 16.0
GLA_CHUNK = 64
GLA_SUPER = 256
GLA_HEADS_PER_STEP = 2
NEG_MASK = -1e30
LOG2_E = 1.4426950408889634

V7X_VMEM_LIMIT_BYTES = 58 * 1024 * 1024


def _params(*sem):
    return pltpu.CompilerParams(dimension_semantics=sem, vmem_limit_bytes=V7X_VMEM_LIMIT_BYTES)


def _silu(z):
    return z / (1.0 + jnp.exp(-z))


def _dot(a, b):
    return jnp.dot(a, b, preferred_element_type=F32)


def _dot_nt(a, b):
    return lax.dot_general(a, b, (((1,), (1,)), ((), ())), preferred_element_type=F32)


def _dot_tn(a, b):
    return lax.dot_general(a, b, (((0,), (0,)), ((), ())), preferred_element_type=F32)


def _split_bf16(a):
    hi = a.astype(BF16)
    lo = (a - hi.astype(F32)).astype(BF16)
    return hi, lo


def _in_proj_kernel(*refs, has_aux, row_chunk):
    if has_aux:
        x_ref, g_ref, w_ref, wa_ref, o_ref, aux_ref, h_ref = refs
    else:
        x_ref, g_ref, w_ref, o_ref, h_ref = refs

    @pl.when(pl.program_id(1) == 0)
    def _():
        def body(c, carry):
            r = pl.multiple_of(c * row_chunk, row_chunk)
            x = x_ref[pl.ds(r, row_chunk), :]
            ms = jnp.mean(x * x, axis=-1, keepdims=True)
            hn = x * lax.rsqrt(ms + RMS_EPS) * g_ref[...]
            h_ref[pl.ds(r, row_chunk), :] = hn.astype(BF16)
            if has_aux:
                h_hi, h_lo = _split_bf16(hn)
                w_hi = wa_ref[0]
                w_lo = wa_ref[1]
                aux_ref[pl.ds(r, row_chunk), :] = _dot(h_hi, w_hi) + _dot(h_lo, w_hi) + _dot(h_hi, w_lo)
            return carry
        lax.fori_loop(0, x_ref.shape[0] // row_chunk, body, 0)

    o_ref[...] = _dot(h_ref[...], w_ref[...]).astype(o_ref.dtype)


def _in_proj(x2d, g, w_bf16, w_aux=None):
    t, d = x2d.shape
    n = w_bf16.shape[1]
    tm = min(1024, t)
    tn = min(2048, n)
    assert t % tm == 0 and n % tn == 0 and tm % 128 == 0
    has_aux = w_aux is not None
    in_specs = [
        pl.BlockSpec((tm, d), lambda i, j: (i, 0)),
        pl.BlockSpec((1, d), lambda i, j: (0, 0)),
        pl.BlockSpec((d, tn), lambda i, j: (0, j)),
    ]
    args = [x2d, g.reshape(1, d), w_bf16]
    out_shape = [jax.ShapeDtypeStruct((t, n), BF16)]
    out_specs = [pl.BlockSpec((tm, tn), lambda i, j: (i, j))]
    if has_aux:
        na = w_aux.shape[1]
        hi, lo = _split_bf16(w_aux)
        in_specs.append(pl.BlockSpec((2, d, na), lambda i, j: (0, 0, 0)))
        args.append(jnp.stack([hi, lo]))
        out_shape.append(jax.ShapeDtypeStruct((t, na), F32))
        out_specs.append(pl.BlockSpec((tm, na), lambda i, j: (i, 0)))
    res = pl.pallas_call(
        functools.partial(_in_proj_kernel, has_aux=has_aux, row_chunk=128),
        grid=(t // tm, n // tn),
        in_specs=in_specs,
        out_specs=out_specs,
        out_shape=out_shape,
        scratch_shapes=[pltpu.VMEM((tm, d), BF16)],
        compiler_params=_params("parallel", "arbitrary"),
    )(*args)
    return res if has_aux else res[0]


def _out_proj_kernel(a_ref, w_ref, g_ref, x_ref, o_ref):
    t = _dot(a_ref[...], w_ref[...])
    ms = jnp.mean(t * t, axis=-1, keepdims=True)
    o_ref[...] = x_ref[...] + t * lax.rsqrt(ms + RMS_EPS) * g_ref[...]


def _out_proj(a2d, w_bf16, g, x2d):
    t, e = a2d.shape
    d = w_bf16.shape[1]
    tm = min(512, t)
    assert t % tm == 0
    return pl.pallas_call(
        _out_proj_kernel,
        grid=(t // tm,),
        in_specs=[pl.BlockSpec((tm, e), lambda i: (i, 0)),
                  pl.BlockSpec((e, d), lambda i: (0, 0), pipeline_mode=pl.Buffered(1)),
                  pl.BlockSpec((1, d), lambda i: (0, 0)),
                  pl.BlockSpec((tm, d), lambda i: (i, 0))],
        out_specs=pl.BlockSpec((tm, d), lambda i: (i, 0)),
        out_shape=jax.ShapeDtypeStruct((t, d), F32),
        compiler_params=_params("parallel"),
    )(a2d, w_bf16, g.reshape(1, d), x2d)


def _gated_out_proj_kernel(*refs, n_z, head_norm_width):
    m_ref = refs[0]
    z_refs = refs[1:1 + n_z]
    if head_norm_width:
        gn_ref, w_ref, g_ref, x_ref, o_ref = refs[1 + n_z:]
    else:
        w_ref, g_ref, x_ref, o_ref = refs[1 + n_z:]
    e = m_ref.shape[1]
    zw = e // n_z
    cw = head_norm_width if head_norm_width else min(zw, 1024)
    t = None
    for c in range(e // cw):
        sl = slice(c * cw, (c + 1) * cw)
        m = m_ref[:, sl].astype(F32)
        if head_norm_width:
            m = m * lax.rsqrt(jnp.mean(m * m, axis=-1, keepdims=True) + RMS_EPS) * gn_ref[...]
        zi, zo = divmod(c * cw, zw)
        z = z_refs[zi][:, zo:zo + cw].astype(F32)
        part = _dot((m * _silu(z)).astype(BF16), w_ref[sl, :])
        t = part if t is None else t + part
    ms = jnp.mean(t * t, axis=-1, keepdims=True)
    o_ref[...] = x_ref[...] + t * lax.rsqrt(ms + RMS_EPS) * g_ref[...]


def _gated_out_proj(m2d, proj, z_start, w_bf16, g, x2d, head_gain=None):
    t, e = m2d.shape
    d = w_bf16.shape[1]
    tm = min(256, t)
    n_z = 2
    zw = e // n_z
    assert t % tm == 0 and z_start % zw == 0
    zb = z_start // zw
    in_specs = [pl.BlockSpec((tm, e), lambda i: (i, 0))]
    in_specs += [pl.BlockSpec((tm, zw), lambda i, j=j: (i, zb + j)) for j in range(n_z)]
    args = [m2d] + [proj] * n_z
    hw = 0
    if head_gain is not None:
        hw = head_gain.shape[0]
        assert zw % hw == 0
        in_specs.append(pl.BlockSpec((1, hw), lambda i: (0, 0)))
        args.append(head_gain.reshape(1, hw).astype(F32))
    in_specs += [pl.BlockSpec((e, d), lambda i: (0, 0), pipeline_mode=pl.Buffered(1)),
                 pl.BlockSpec((1, d), lambda i: (0, 0)),
                 pl.BlockSpec((tm, d), lambda i: (i, 0))]
    args += [w_bf16, g.reshape(1, d), x2d]
    return pl.pallas_call(
        functools.partial(_gated_out_proj_kernel, n_z=n_z, head_norm_width=hw),
        grid=(t // tm,),
        in_specs=in_specs,
        out_specs=pl.BlockSpec((tm, d), lambda i: (i, 0)),
        out_shape=jax.ShapeDtypeStruct((t, d), F32),
        compiler_params=_params("parallel"),
    )(*args)


def _dft_tables(n1, gb):
    n2 = DFT_N2
    n = n1 * n2
    k1 = np.arange(n1)
    ang_a = 2.0 * np.pi * np.outer(k1, k1) / n1
    eye = np.eye(gb)
    fa_re = np.kron(eye, np.cos(ang_a))
    fa_im = np.kron(eye, -np.sin(ang_a))
    fa = np.concatenate([fa_re, fa_im], axis=0) / math.sqrt(n1)

    kk1 = jnp.arange(n1, dtype=jnp.int32)[:, None, None]
    kk2 = jnp.arange(n2, dtype=jnp.int32)[None, :, None]
    nn2 = jnp.arange(n2, dtype=jnp.int32)[None, None, :]
    m = (nn2 * (kk1 + n1 * kk2)) % n
    ang = m.astype(F32) * (2.0 * math.pi / n)
    c = jnp.cos(ang) / math.sqrt(n2)
    s = jnp.sin(ang) / math.sqrt(n2)
    mtab = jnp.concatenate([jnp.concatenate([c, s], axis=2), jnp.concatenate([-s, c], axis=2)], axis=1)

    cw = FNET_GROUP_W
    ang_c = 2.0 * np.pi * (np.outer(np.arange(cw), np.arange(cw)) % cw) / cw
    cc = np.cos(ang_c) / math.sqrt(cw)
    sc = np.sin(ang_c) / math.sqrt(cw)
    return (jnp.asarray(fa, F32).astype(BF16), mtab.astype(BF16),
            jnp.asarray(cc, F32).astype(BF16), jnp.asarray(sc, F32).astype(BF16))


def _dft_a_kernel(f_ref, u_ref, o_ref):
    o_ref[...] = _dot(f_ref[...], u_ref[...]).astype(o_ref.dtype)


def _dft_c_kernel(a_ref, m_ref, cc_ref, sc_ref, z_ref, o_ref, zr_ref, zi_ref, *, rb, cb):
    n2 = DFT_N2
    for j in range(rb):
        zz = _dot(m_ref[j], a_ref[j])
        zr_ref[j * n2:(j + 1) * n2, :] = zz[:n2].astype(BF16)
        zi_ref[j * n2:(j + 1) * n2, :] = zz[n2:].astype(BF16)
    cw = FNET_GROUP_W
    for gi in range(cb // cw):
        sl = slice(gi * cw, (gi + 1) * cw)
        y = _dot(zr_ref[:, sl], cc_ref[...]) + _dot(zi_ref[:, sl], sc_ref[...])
        z = z_ref[:, sl].astype(F32)
        o_ref[:, sl] = (y * _silu(z)).astype(o_ref.dtype)


def _fnet_mixer(uz, b, s):
    e = uz.shape[1] // 2
    n2 = DFT_N2
    assert s % n2 == 0
    n1 = s // n2
    gb = max(1, min(b, 256 // n1))
    while b % gb:
        gb -= 1
    rg = gb * n1
    ng = b // gb
    fa, mtab, cc, sc = _dft_tables(n1, gb)

    u_t = jnp.transpose(uz.reshape(b, n1, n2, 2 * e)[..., :e], (2, 0, 1, 3)).reshape(n2, ng, rg, e)
    ca = min(4096, e)
    a_nat = pl.pallas_call(
        _dft_a_kernel,
        grid=(n2, ng, e // ca),
        in_specs=[pl.BlockSpec((2 * rg, rg), lambda n, g, c: (0, 0)),
                  pl.BlockSpec((None, None, rg, ca), lambda n, g, c: (n, g, 0, c))],
        out_specs=pl.BlockSpec((None, None, 2 * rg, ca), lambda n, g, c: (n, g, 0, c)),
        out_shape=jax.ShapeDtypeStruct((n2, ng, 2 * rg, e), BF16),
        compiler_params=_params("parallel", "parallel", "parallel"),
    )(fa, u_t)

    r = b * n1
    a3 = jnp.transpose(a_nat.reshape(n2, ng, 2, rg, e), (1, 3, 2, 0, 4)).reshape(r, 2 * n2, e)
    z_t = jnp.transpose(uz.reshape(b, n2, n1, 2 * e)[..., e:], (0, 2, 1, 3)).reshape(r * n2, e)

    rb = min(8, n1)
    assert n1 % rb == 0
    cb = min(1024, e)
    nkb = n1 // rb
    a_t = pl.pallas_call(
        functools.partial(_dft_c_kernel, rb=rb, cb=cb),
        grid=(r // rb, e // cb),
        in_specs=[pl.BlockSpec((rb, 2 * n2, cb), lambda i, c: (i, 0, c)),
                  pl.BlockSpec((rb, 2 * n2, 2 * n2), lambda i, c: (i % nkb, 0, 0)),
                  pl.BlockSpec((FNET_GROUP_W, FNET_GROUP_W), lambda i, c: (0, 0)),
                  pl.BlockSpec((FNET_GROUP_W, FNET_GROUP_W), lambda i, c: (0, 0)),
                  pl.BlockSpec((rb * n2, cb), lambda i, c: (i, c))],
        out_specs=pl.BlockSpec((rb * n2, cb), lambda i, c: (i, c)),
        out_shape=jax.ShapeDtypeStruct((r * n2, e), BF16),
        scratch_shapes=[pltpu.VMEM((rb * n2, cb), BF16), pltpu.VMEM((rb * n2, cb), BF16)],
        compiler_params=_params("parallel", "arbitrary"),
    )(a3, mtab, cc, sc, z_t)
    return jnp.transpose(a_t.reshape(b, n1, n2, e), (0, 2, 1, 3)).reshape(b * s, e)


def _nat_bias_pairs(rpb):
    qc = np.arange(GRID_W)[:, None]
    kc = np.arange(GRID_W)[None, :]
    ws = np.clip(qc - NAT_WIN_W // 2, 0, GRID_W - NAT_WIN_W)
    valid = (kc >= ws) & (kc < ws + NAT_WIN_W)
    rel = np.clip(kc - qc, -(NAT_WIN_W - 1), NAT_WIN_W - 1) + NAT_WIN_W - 1
    colb = jnp.where(jnp.asarray(valid)[None, None], rpb[:, :, jnp.asarray(rel)].astype(F32), NEG_MASK)
    return jnp.concatenate([colb[:, :-1], colb[:, 1:]], axis=-1) * LOG2_E


def _nat_kernel(*refs, n_rb, heads):
    q_ref = refs[0]
    k_refs = refs[1:1 + NAT_N_SUB]
    v_refs = refs[1 + NAT_N_SUB:1 + 2 * NAT_N_SUB]
    bias_ref, o_ref, s_ref, p_ref = refs[1 + 2 * NAT_N_SUB:]
    hd = NAT_HEAD_DIM
    w = GRID_W
    nq = NAT_Q_ROWS * w
    nks = NAT_KEY_SUB * w
    lane = lax.broadcasted_iota(jnp.int32, (w, 2 * w), 1)

    rows_half = NAT_Q_ROWS // 2
    hq = rows_half * w
    subs_half = NAT_N_SUB - 1
    pairs_sub = NAT_KEY_SUB // 2

    def run(window_starts):
        def scores(h, slot):
            off = pl.multiple_of(h * hd, hd)
            ks = [k_refs[j][:, :, pl.ds(off, hd)].reshape(nks, hd) for j in range(NAT_N_SUB)]
            for a in range(2):
                q = q_ref[a * rows_half:(a + 1) * rows_half, :, pl.ds(off, hd)].reshape(hq, hd)
                for j in range(a, a + subs_half):
                    s_ref[slot, a * hq:(a + 1) * hq, j * nks:(j + 1) * nks] = _dot_nt(q, ks[j])

        def softmax(h, slot):
            for i in range(NAT_Q_ROWS):
                lo = window_starts[i]
                kp_lo, kp_hi = lo // 2, (lo + NAT_WIN_H - 1) // 2
                a = i // rows_half
                assert a * pairs_sub <= kp_lo and kp_hi < (a + subs_half) * pairs_sub
                rows = slice(i * w, (i + 1) * w)
                tiles = []
                for kp in range(kp_lo, kp_hi + 1):
                    t = s_ref[slot, rows, kp * 2 * w:(kp + 1) * 2 * w] + bias_ref[h, 2 * kp - i + 3]
                    if 2 * kp < lo:
                        t = jnp.where(lane >= w, t, NEG_MASK)
                    if 2 * kp + 1 >= lo + NAT_WIN_H:
                        t = jnp.where(lane < w, t, NEG_MASK)
                    tiles.append(t)
                m = tiles[0]
                for t in tiles[1:]:
                    m = jnp.maximum(m, t)
                m = jnp.max(m, axis=-1, keepdims=True)
                es = [jnp.exp2(t - m) for t in tiles]
                tot = es[0]
                for ee in es[1:]:
                    tot = tot + ee
                inv = 1.0 / jnp.sum(tot, axis=-1, keepdims=True)
                for kp in range(a * pairs_sub, (a + subs_half) * pairs_sub):
                    cols = slice(kp * 2 * w, (kp + 1) * 2 * w)
                    if kp_lo <= kp <= kp_hi:
                        p_ref[slot, rows, cols] = (es[kp - kp_lo] * inv).astype(BF16)
                    else:
                        p_ref[slot, rows, cols] = jnp.zeros((w, 2 * w), BF16)

        def weighted_sum(h, slot):
            off = pl.multiple_of(h * hd, hd)
            vs = [v_refs[j][:, :, pl.ds(off, hd)].reshape(nks, hd) for j in range(NAT_N_SUB)]
            for a in range(2):
                acc = None
                for j in range(a, a + subs_half):
                    part = _dot(p_ref[slot, a * hq:(a + 1) * hq, j * nks:(j + 1) * nks], vs[j])
                    acc = part if acc is None else acc + part
                qrows = slice(a * rows_half, (a + 1) * rows_half)
                o_ref[qrows, :, pl.ds(off, hd)] = acc.astype(o_ref.dtype).reshape(rows_half, w, hd)

        def group_body(t, carry):
            hs = [NAT_HEAD_GROUP * t + g for g in range(NAT_HEAD_GROUP)]
            for g, h in enumerate(hs):
                scores(h, g)
            for g, h in enumerate(hs):
                softmax(h, g)
            for g, h in enumerate(hs):
                weighted_sum(h, g)
            return carry
        lax.fori_loop(0, heads // NAT_HEAD_GROUP, group_body, 0)

    rb = pl.program_id(2)
    half = NAT_WIN_H // 2
    interior = list(range(NAT_Q_ROWS))
    top = [max(i, half) for i in range(NAT_Q_ROWS)]
    bottom = [min(i, half) for i in range(NAT_Q_ROWS)]

    @pl.when(rb == 0)
    def _():
        run(top)

    @pl.when(rb == n_rb - 1)
    def _():
        run(bottom)

    @pl.when(jnp.logical_and(rb > 0, rb < n_rb - 1))
    def _():
        run(interior)


def _nat_mixer(qkvz, rpb, b, s):
    e = qkvz.shape[1] // 4
    w = GRID_W
    rows = s // w
    assert s % w == 0 and rows % NAT_Q_ROWS == 0 and rows >= 2 * NAT_Q_ROWS
    n_rb = rows // NAT_Q_ROWS
    hb = 8
    lw = hb * NAT_HEAD_DIM
    nhg = e // lw
    x4 = qkvz.reshape(b, rows, w, 4 * e)
    bias = _nat_bias_pairs(rpb)
    n_kblk = rows // NAT_KEY_SUB

    def kv_spec(j, sec):
        def imap(g, bi, r):
            blk = jnp.clip(2 * r - 1 + j, 0, n_kblk - 1)
            return (bi, blk, 0, sec * nhg + g)
        return pl.BlockSpec((None, NAT_KEY_SUB, w, lw), imap)

    in_specs = ([pl.BlockSpec((None, NAT_Q_ROWS, w, lw), lambda g, bi, r: (bi, r, 0, g))]
                + [kv_spec(j, 1) for j in range(NAT_N_SUB)]
                + [kv_spec(j, 2) for j in range(NAT_N_SUB)]
                + [pl.BlockSpec((hb, 2 * NAT_WIN_H - 2, w, 2 * w), lambda g, bi, r: (g, 0, 0, 0))])
    nq = NAT_Q_ROWS * w
    nk = NAT_N_SUB * NAT_KEY_SUB * w
    out = pl.pallas_call(
        functools.partial(_nat_kernel, n_rb=n_rb, heads=hb),
        grid=(nhg, b, n_rb),
        in_specs=in_specs,
        out_specs=pl.BlockSpec((None, NAT_Q_ROWS, w, lw), lambda g, bi, r: (bi, r, 0, g)),
        out_shape=jax.ShapeDtypeStruct((b, rows, w, e), BF16),
        scratch_shapes=[pltpu.VMEM((NAT_HEAD_GROUP, nq, nk), F32),
                        pltpu.VMEM((NAT_HEAD_GROUP, nq, nk), BF16)],
        compiler_params=_params("parallel", "parallel", "arbitrary"),
    )(*([x4] * (1 + 2 * NAT_N_SUB)), bias)
    return out.reshape(b * s, e)


def _log_gate(g1, wa2_ref, ba_ref):
    g_hi, g_lo = _split_bf16(g1)
    w_hi = wa2_ref[0]
    w_lo = wa2_ref[1]
    pre = _dot(g_hi, w_hi) + _dot(g_lo, w_hi) + _dot(g_hi, w_lo) + ba_ref[...]
    return (jnp.minimum(pre, 0.0) - jnp.log(1.0 + jnp.exp(-jnp.abs(pre)))) / GLA_GATE_TEMP


def _gla_superchunk(q, k, v, la, state_ref, reverse):
    n, dk = q.shape
    c = GLA_CHUNK
    nb = n // c
    ii = lax.broadcasted_iota(jnp.int32, (n, n), 0)
    jj = lax.broadcasted_iota(jnp.int32, (n, n), 1)
    tri = jnp.where((jj >= ii) if reverse else (jj <= ii), 1.0, 0.0).astype(BF16)
    la_hi, la_lo = _split_bf16(la)
    bc = _dot(tri, la_hi) + _dot(tri, la_lo)
    qf = q.astype(F32) * (dk ** -0.5)
    kf = k.astype(F32)

    def brow(idx):
        return bc[idx:idx + 1]

    ka_parts = []
    mids = []
    for blk in range(nb):
        r0 = blk * c
        b_mid = brow(r0 + c // 2) if reverse else brow(r0 + c // 2 - 1)
        mids.append(b_mid)
        ka_parts.append((kf[r0:r0 + c] * jnp.exp(b_mid - bc[r0:r0 + c])).astype(BF16))
    ka = jnp.concatenate(ka_parts, axis=0)

    row_i = lax.broadcasted_iota(jnp.int32, (c, n), 0)
    col_j = lax.broadcasted_iota(jnp.int32, (c, n), 1)
    s_rows = []
    for blk in range(nb):
        r0 = blk * c
        b_blk = bc[r0:r0 + c]
        q_blk = qf[r0:r0 + c]
        qa = (q_blk * jnp.exp(b_blk - mids[blk])).astype(BF16)
        s_diag = _dot_nt(qa, ka)
        in_blk = jnp.logical_and(col_j >= r0, col_j < r0 + c)
        if reverse:
            keep = jnp.logical_and(in_blk, col_j > row_i + r0)
        else:
            keep = jnp.logical_and(in_blk, col_j <= row_i + r0)
        s_blk = jnp.where(keep, s_diag, 0.0)
        has_other = blk < nb - 1 if reverse else blk > 0
        if has_other:
            b_s = brow(r0 + c) if reverse else brow(r0 - 1)
            qo = (q_blk * jnp.exp(b_blk - b_s)).astype(BF16)
            if reverse:
                ko = (kf[r0 + c:] * jnp.exp(b_s - bc[r0 + c:])).astype(BF16)
                ko = jnp.concatenate([jnp.zeros((r0 + c, dk), BF16), ko], axis=0)
            else:
                ko = (kf[:r0] * jnp.exp(b_s - bc[:r0])).astype(BF16)
                ko = jnp.concatenate([ko, jnp.zeros((n - r0, dk), BF16)], axis=0)
            s_blk = s_blk + _dot_nt(qo, ko)
        s_rows.append(s_blk.astype(BF16))
    scores = jnp.concatenate(s_rows, axis=0)

    b_end = brow(0) if reverse else brow(n - 1)
    q_in = (qf * jnp.exp(bc)).astype(BF16)
    o = _dot(scores, v) + _dot(q_in, state_ref[...].astype(BF16))
    k_out = (kf * jnp.exp(b_end - bc)).astype(BF16)
    ones = jnp.ones((n, 128), BF16)
    decay = jnp.exp(_dot_tn(la_hi, ones) + _dot_tn(la_lo, ones))
    upd = _dot_tn(k_out, v)
    for lb in range(v.shape[1] // 128):
        sl = slice(lb * 128, (lb + 1) * 128)
        state_ref[:, sl] = decay * state_ref[:, sl] + upd[:, sl]
    return o


def _gla_fwd_kernel(q_ref, k_ref, v_ref, g1_ref, wa2_ref, ba_ref, o_ref, state_ref):
    @pl.when(pl.program_id(2) == 0)
    def _():
        state_ref[...] = jnp.zeros_like(state_ref)

    n_heads, dk, dv = state_ref.shape
    la = _log_gate(g1_ref[:, :GLA_GATE_RANK], wa2_ref, ba_ref)
    n = GLA_SUPER
    for sc in range(q_ref.shape[0] // n):
        rows = slice(sc * n, (sc + 1) * n)
        for g in range(n_heads):
            kl = slice(g * dk, (g + 1) * dk)
            vl = slice(g * dv, (g + 1) * dv)
            o = _gla_superchunk(q_ref[rows, kl], k_ref[rows, kl], v_ref[rows, vl], la[rows, kl],
                                state_ref.at[g], False)
            o_ref[rows, vl] = o.astype(o_ref.dtype)


def _gla_bwd_kernel(q_ref, k_ref, v_ref, g1_ref, wa2_ref, ba_ref, of_ref, o_ref, state_ref):
    @pl.when(pl.program_id(2) == 0)
    def _():
        state_ref[...] = jnp.zeros_like(state_ref)

    n_heads, dk, dv = state_ref.shape
    la = _log_gate(g1_ref[:, GLA_GATE_RANK:], wa2_ref, ba_ref)
    n = GLA_SUPER
    for sc in reversed(range(q_ref.shape[0] // n)):
        rows = slice(sc * n, (sc + 1) * n)
        for g in range(n_heads):
            kl = slice(g * dk, (g + 1) * dk)
            vl = slice(g * dv, (g + 1) * dv)
            o = _gla_superchunk(q_ref[rows, kl], k_ref[rows, kl], v_ref[rows, vl], la[rows, kl],
                                state_ref.at[g], True)
            o_ref[rows, vl] = (o + of_ref[rows, vl].astype(F32)).astype(o_ref.dtype)


def _gla_mixer(qkvz, g1, wa2_f, ba_f, wa2_b, ba_b, g_norm, b, s):
    hh = GLA_HEADS
    dv = g_norm.shape[0]
    e = hh * dv
    kd = (qkvz.shape[1] - 2 * e) // 2
    dk = kd // hh
    hp = GLA_HEADS_PER_STEP
    wk, wv = hp * dk, hp * dv
    assert hh % hp == 0 and kd % wk == 0 and (2 * kd) % wv == 0
    tb = min(1024, s)
    assert s % tb == 0 and tb % GLA_SUPER == 0
    nt = s // tb
    x3 = qkvz.reshape(b, s, qkvz.shape[1])
    g3 = g1.reshape(b, s, 2 * GLA_GATE_RANK)
    k0 = kd // wk
    v0 = 2 * kd // wv

    def specs(tmap):
        return [pl.BlockSpec((None, tb, wk), lambda bi, h, t: (bi, tmap(t), h)),
                pl.BlockSpec((None, tb, wk), lambda bi, h, t: (bi, tmap(t), k0 + h)),
                pl.BlockSpec((None, tb, wv), lambda bi, h, t: (bi, tmap(t), v0 + h)),
                pl.BlockSpec((None, tb, 2 * GLA_GATE_RANK), lambda bi, h, t: (bi, tmap(t), 0)),
                pl.BlockSpec((2, GLA_GATE_RANK, wk), lambda bi, h, t: (0, 0, h)),
                pl.BlockSpec((1, wk), lambda bi, h, t: (0, h))]

    def gate_w(wa2):
        hi, lo = _split_bf16(wa2)
        return jnp.stack([hi, lo])

    fwd = lambda t: t
    o_f = pl.pallas_call(
        _gla_fwd_kernel,
        grid=(b, hh // hp, nt),
        in_specs=specs(fwd),
        out_specs=pl.BlockSpec((None, tb, wv), lambda bi, h, t: (bi, t, h)),
        out_shape=jax.ShapeDtypeStruct((b, s, e), BF16),
        scratch_shapes=[pltpu.VMEM((hp, dk, dv), F32)],
        compiler_params=_params("parallel", "parallel", "arbitrary"),
    )(x3, x3, x3, g3, gate_w(wa2_f), ba_f.reshape(1, kd))

    rev = lambda t: nt - 1 - t
    out = pl.pallas_call(
        _gla_bwd_kernel,
        grid=(b, hh // hp, nt),
        in_specs=specs(rev) + [pl.BlockSpec((None, tb, wv), lambda bi, h, t: (bi, rev(t), h))],
        out_specs=pl.BlockSpec((None, tb, wv), lambda bi, h, t: (bi, rev(t), h)),
        out_shape=jax.ShapeDtypeStruct((b, s, e), BF16),
        scratch_shapes=[pltpu.VMEM((hp, dk, dv), F32)],
        compiler_params=_params("parallel", "parallel", "arbitrary"),
    )(x3, x3, x3, g3, gate_w(wa2_b), ba_b.reshape(1, kd), o_f)
    return out.reshape(b * s, e)


def _fnet_layer(x, g_pre, g_post, w_in, w_out):
    b, s, d = x.shape
    x2 = x.reshape(b * s, d)
    uz = _in_proj(x2, g_pre, w_in.astype(BF16))
    a = _fnet_mixer(uz, b, s)
    return _out_proj(a, w_out.astype(BF16), g_post, x2).reshape(b, s, d)


def _nat_layer(x, g_pre, g_post, w_in, rpb, w_out):
    b, s, d = x.shape
    x2 = x.reshape(b * s, d)
    e = w_in.shape[1] // 4
    q_cols = (jnp.arange(w_in.shape[1]) < e)[None, :]
    w_scaled = jnp.where(q_cols, w_in * (NAT_HEAD_DIM ** -0.5 * LOG2_E), w_in)
    qkvz = _in_proj(x2, g_pre, w_scaled.astype(BF16))
    m = _nat_mixer(qkvz, rpb, b, s)
    return _gated_out_proj(m, qkvz, 3 * e, w_out.astype(BF16), g_post, x2).reshape(b, s, d)


def _gla_layer(x, g_pre, g_post, w_in, wa1_f, wa2_f, ba_f, wa1_b, wa2_b, ba_b, g_norm, w_out):
    b, s, d = x.shape
    x2 = x.reshape(b * s, d)
    qkvz, g1 = _in_proj(x2, g_pre, w_in.astype(BF16), jnp.concatenate([wa1_f, wa1_b], axis=1))
    m = _gla_mixer(qkvz, g1, wa2_f, ba_f, wa2_b, ba_b, g_norm, b, s)
    z_start = qkvz.shape[1] - m.shape[1]
    return _gated_out_proj(m, qkvz, z_start, w_out.astype(BF16), g_post, x2,
                           head_gain=g_norm).reshape(b, s, d)


def _trunk(x, norm_pre_g, norm_post_g, fnet_w_in, fnet_w_out, nat_w_in, nat_rpb, nat_w_out,
           gla_w_in, gla_wa1_f, gla_wa2_f, gla_ba_f, gla_wa1_b, gla_wa2_b, gla_ba_b, gla_g_norm,
           gla_w_out):
    depth = norm_pre_g.shape[0]
    for i in range(depth):
        m, j = i % 3, i // 3
        if m == 0:
            x = _fnet_layer(x, norm_pre_g[i], norm_post_g[i], fnet_w_in[j], fnet_w_out[j])
        elif m == 1:
            x = _nat_layer(x, norm_pre_g[i], norm_post_g[i], nat_w_in[j], nat_rpb[j], nat_w_out[j])
        else:
            x = _gla_layer(x, norm_pre_g[i], norm_post_g[i], gla_w_in[j], gla_wa1_f[j], gla_wa2_f[j],
                           gla_ba_f[j], gla_wa1_b[j], gla_wa2_b[j], gla_ba_b[j], gla_g_norm[j],
                           gla_w_out[j])
    return x


def kernel(x_prompt, x_sample, norm_pre_g, norm_post_g, fnet_w_in, fnet_w_out, nat_w_in, nat_rpb,
           nat_w_out, gla_w_in, gla_wa1_f, gla_wa2_f, gla_ba_f, gla_wa1_b, gla_wa2_b, gla_ba_b,
           gla_g_norm, gla_w_out):
    params = (norm_pre_g, norm_post_g, fnet_w_in, fnet_w_out, nat_w_in, nat_rpb, nat_w_out,
              gla_w_in, gla_wa1_f, gla_wa2_f, gla_ba_f, gla_wa1_b, gla_wa2_b, gla_ba_b, gla_g_norm,
              gla_w_out)
    return (_trunk(x_prompt, *params), _trunk(x_sample, *params))
```

```python
import functools
import math

import numpy as np
import jax
import jax.numpy as jnp
from jax import lax
from jax.experimental import pallas as pl
from jax.experimental.pallas import tpu as pltpu

F32 = jnp.float32
BF16 = jnp.bfloat16

RMS_EPS = 1e-6
GRID_W = 64
FNET_GROUP_W = 512
DFT_N2 = 128
NAT_HEAD_DIM = 128
NAT_WIN_H = 8
NAT_WIN_W = 16
NAT_Q_ROWS = 8
NAT_KEY_SUB = 4
NAT_N_SUB = 4
NAT_HEAD_GROUP = 8
GLA_HEADS = 4
GLA_GATE_RANK = 16
GLA_GATE_TEMP = 16.0
GLA_CHUNK = 64
GLA_SUPER = 256
GLA_HEADS_PER_STEP = 2
NEG_MASK = -1e30
LOG2_E = 1.4426950408889634

V7X_VMEM_LIMIT_BYTES = 58 * 1024 * 1024


def _params(*sem):
    return pltpu.CompilerParams(dimension_semantics=sem, vmem_limit_bytes=V7X_VMEM_LIMIT_BYTES)


def _silu(z):
    return z / (1.0 + jnp.exp(-z))


def _dot(a, b):
    return jnp.dot(a, b, preferred_element_type=F32)


def _dot_nt(a, b):
    return lax.dot_general(a, b, (((1,), (1,)), ((), ())), preferred_element_type=F32)


def _dot_tn(a, b):
    return lax.dot_general(a, b, (((0,), (0,)), ((), ())), preferred_element_type=F32)


def _split_bf16(a):
    hi = a.astype(BF16)
    lo = (a - hi.astype(F32)).astype(BF16)
    return hi, lo


def _in_proj_kernel(*refs, has_aux, row_chunk):
    if has_aux:
        x_ref, g_ref, w_ref, wa_ref, o_ref, aux_ref, h_ref = refs
    else:
        x_ref, g_ref, w_ref, o_ref, h_ref = refs

    @pl.when(pl.program_id(1) == 0)
    def _():
        for c in range(x_ref.shape[0] // row_chunk):
            rows = slice(c * row_chunk, (c + 1) * row_chunk)
            x = x_ref[rows, :]
            ms = jnp.mean(x * x, axis=-1, keepdims=True)
            hn = x * lax.rsqrt(ms + RMS_EPS) * g_ref[...]
            hb = hn.astype(BF16)
            h_ref[rows, :] = hb
            if has_aux:
                h_hi, h_lo = _split_bf16(hn)
                w_hi = wa_ref[0]
                w_lo = wa_ref[1]
                aux_ref[rows, :] = _dot(h_hi, w_hi) + _dot(h_lo, w_hi) + _dot(h_hi, w_lo)
            o_ref[rows, :] = _dot(hb, w_ref[...]).astype(o_ref.dtype)

    @pl.when(pl.program_id(1) > 0)
    def _():
        o_ref[...] = _dot(h_ref[...], w_ref[...]).astype(o_ref.dtype)


def _in_proj(x2d, g, w_bf16, w_aux=None):
    t, d = x2d.shape
    n = w_bf16.shape[1]
    tm = min(1024, t)
    tn = min(2048, n)
    assert t % tm == 0 and n % tn == 0 and tm % 128 == 0
    has_aux = w_aux is not None
    in_specs = [
        pl.BlockSpec((tm, d), lambda i, j: (i, 0)),
        pl.BlockSpec((1, d), lambda i, j: (0, 0)),
        pl.BlockSpec((d, tn), lambda i, j: (0, j)),
    ]
    args = [x2d, g.reshape(1, d), w_bf16]
    out_shape = [jax.ShapeDtypeStruct((t, n), BF16)]
    out_specs = [pl.BlockSpec((tm, tn), lambda i, j: (i, j))]
    if has_aux:
        na = w_aux.shape[1]
        hi, lo = _split_bf16(w_aux)
        in_specs.append(pl.BlockSpec((2, d, na), lambda i, j: (0, 0, 0)))
        args.append(jnp.stack([hi, lo]))
        out_shape.append(jax.ShapeDtypeStruct((t, na), F32))
        out_specs.append(pl.BlockSpec((tm, na), lambda i, j: (i, 0)))
    res = pl.pallas_call(
        functools.partial(_in_proj_kernel, has_aux=has_aux, row_chunk=min(256, tm)),
        grid=(t // tm, n // tn),
        in_specs=in_specs,
        out_specs=out_specs,
        out_shape=out_shape,
        scratch_shapes=[pltpu.VMEM((tm, d), BF16)],
        compiler_params=_params("parallel", "arbitrary"),
    )(*args)
    return res if has_aux else res[0]


def _out_proj_kernel(a_ref, w_ref, g_ref, x_ref, o_ref):
    t = _dot(a_ref[...], w_ref[...])
    ms = jnp.mean(t * t, axis=-1, keepdims=True)
    o_ref[...] = x_ref[...] + t * lax.rsqrt(ms + RMS_EPS) * g_ref[...]


def _out_proj(a2d, w_bf16, g, x2d):
    t, e = a2d.shape
    d = w_bf16.shape[1]
    tm = min(512, t)
    assert t % tm == 0
    return pl.pallas_call(
        _out_proj_kernel,
        grid=(t // tm,),
        in_specs=[pl.BlockSpec((tm, e), lambda i: (i, 0)),
                  pl.BlockSpec((e, d), lambda i: (0, 0), pipeline_mode=pl.Buffered(1)),
                  pl.BlockSpec((1, d), lambda i: (0, 0)),
                  pl.BlockSpec((tm, d), lambda i: (i, 0))],
        out_specs=pl.BlockSpec((tm, d), lambda i: (i, 0)),
        out_shape=jax.ShapeDtypeStruct((t, d), F32),
        compiler_params=_params("parallel"),
    )(a2d, w_bf16, g.reshape(1, d), x2d)


def _gated_out_proj_kernel(*refs, n_z, head_norm_width):
    m_ref = refs[0]
    z_refs = refs[1:1 + n_z]
    if head_norm_width:
        gn_ref, w_ref, g_ref, x_ref, o_ref = refs[1 + n_z:]
    else:
        w_ref, g_ref, x_ref, o_ref = refs[1 + n_z:]
    e = m_ref.shape[1]
    zw = e // n_z
    cw = head_norm_width if head_norm_width else min(zw, 1024)
    t = None
    for c in range(e // cw):
        sl = slice(c * cw, (c + 1) * cw)
        m = m_ref[:, sl].astype(F32)
        if head_norm_width:
            m = m * lax.rsqrt(jnp.mean(m * m, axis=-1, keepdims=True) + RMS_EPS) * gn_ref[...]
        zi, zo = divmod(c * cw, zw)
        z = z_refs[zi][:, zo:zo + cw].astype(F32)
        part = _dot((m * _silu(z)).astype(BF16), w_ref[sl, :])
        t = part if t is None else t + part
    ms = jnp.mean(t * t, axis=-1, keepdims=True)
    o_ref[...] = x_ref[...] + t * lax.rsqrt(ms + RMS_EPS) * g_ref[...]


def _gated_out_proj(m2d, proj, z_start, w_bf16, g, x2d, head_gain=None):
    t, e = m2d.shape
    d = w_bf16.shape[1]
    tm = min(256, t)
    n_z = 2
    zw = e // n_z
    assert t % tm == 0 and z_start % zw == 0
    zb = z_start // zw
    in_specs = [pl.BlockSpec((tm, e), lambda i: (i, 0))]
    in_specs += [pl.BlockSpec((tm, zw), lambda i, j=j: (i, zb + j)) for j in range(n_z)]
    args = [m2d] + [proj] * n_z
    hw = 0
    if head_gain is not None:
        hw = head_gain.shape[0]
        assert zw % hw == 0
        in_specs.append(pl.BlockSpec((1, hw), lambda i: (0, 0)))
        args.append(head_gain.reshape(1, hw).astype(F32))
    in_specs += [pl.BlockSpec((e, d), lambda i: (0, 0), pipeline_mode=pl.Buffered(1)),
                 pl.BlockSpec((1, d), lambda i: (0, 0)),
                 pl.BlockSpec((tm, d), lambda i: (i, 0))]
    args += [w_bf16, g.reshape(1, d), x2d]
    return pl.pallas_call(
        functools.partial(_gated_out_proj_kernel, n_z=n_z, head_norm_width=hw),
        grid=(t // tm,),
        in_specs=in_specs,
        out_specs=pl.BlockSpec((tm, d), lambda i: (i, 0)),
        out_shape=jax.ShapeDtypeStruct((t, d), F32),
        compiler_params=_params("parallel"),
    )(*args)


def _dft_tables(n1, gb):
    n2 = DFT_N2
    n = n1 * n2
    k1 = np.arange(n1)
    ang_a = 2.0 * np.pi * np.outer(k1, k1) / n1
    eye = np.eye(gb)
    fa_re = np.kron(eye, np.cos(ang_a))
    fa_im = np.kron(eye, -np.sin(ang_a))
    fa = np.concatenate([fa_re, fa_im], axis=0) / math.sqrt(n1)

    kk1 = jnp.arange(n1, dtype=jnp.int32)[:, None, None]
    kk2 = jnp.arange(n2, dtype=jnp.int32)[None, :, None]
    nn2 = jnp.arange(n2, dtype=jnp.int32)[None, None, :]
    m = (nn2 * (kk1 + n1 * kk2)) % n
    ang = m.astype(F32) * (2.0 * math.pi / n)
    c = jnp.cos(ang) / math.sqrt(n2)
    s = jnp.sin(ang) / math.sqrt(n2)
    mtab = jnp.concatenate([jnp.concatenate([c, s], axis=2), jnp.concatenate([-s, c], axis=2)], axis=1)

    cw = FNET_GROUP_W
    ang_c = 2.0 * np.pi * (np.outer(np.arange(cw), np.arange(cw)) % cw) / cw
    cc = np.cos(ang_c) / math.sqrt(cw)
    sc = np.sin(ang_c) / math.sqrt(cw)
    return (jnp.asarray(fa, F32).astype(BF16), mtab.astype(BF16),
            jnp.asarray(cc, F32).astype(BF16), jnp.asarray(sc, F32).astype(BF16))


def _dft_a_kernel(f_ref, u_ref, o_ref):
    o_ref[...] = _dot(f_ref[...], u_ref[...]).astype(o_ref.dtype)


def _dft_c_kernel(a_ref, m_ref, cc_ref, sc_ref, z_ref, o_ref, zr_ref, zi_ref, *, rb, cb):
    n2 = DFT_N2
    for j in range(rb):
        zz = _dot(m_ref[j], a_ref[j])
        zr_ref[j * n2:(j + 1) * n2, :] = zz[:n2].astype(BF16)
        zi_ref[j * n2:(j + 1) * n2, :] = zz[n2:].astype(BF16)
    cw = FNET_GROUP_W
    for gi in range(cb // cw):
        sl = slice(gi * cw, (gi + 1) * cw)
        y = _dot(zr_ref[:, sl], cc_ref[...]) + _dot(zi_ref[:, sl], sc_ref[...])
        z = z_ref[:, sl].astype(F32)
        o_ref[:, sl] = (y * _silu(z)).astype(o_ref.dtype)


def _fnet_mixer(uz, b, s):
    e = uz.shape[1] // 2
    n2 = DFT_N2
    assert s % n2 == 0
    n1 = s // n2
    gb = max(1, min(b, 256 // n1))
    while b % gb:
        gb -= 1
    rg = gb * n1
    ng = b // gb
    fa, mtab, cc, sc = _dft_tables(n1, gb)

    u_t = jnp.transpose(uz.reshape(b, n1, n2, 2 * e)[..., :e], (2, 0, 1, 3)).reshape(n2, ng, rg, e)
    ca = min(4096, e)
    a_nat = pl.pallas_call(
        _dft_a_kernel,
        grid=(n2, ng, e // ca),
        in_specs=[pl.BlockSpec((2 * rg, rg), lambda n, g, c: (0, 0)),
                  pl.BlockSpec((None, None, rg, ca), lambda n, g, c: (n, g, 0, c))],
        out_specs=pl.BlockSpec((None, None, 2 * rg, ca), lambda n, g, c: (n, g, 0, c)),
        out_shape=jax.ShapeDtypeStruct((n2, ng, 2 * rg, e), BF16),
        compiler_params=_params("parallel", "parallel", "parallel"),
    )(fa, u_t)

    r = b * n1
    a3 = jnp.transpose(a_nat.reshape(n2, ng, 2, rg, e), (1, 3, 2, 0, 4)).reshape(r, 2 * n2, e)
    z_t = jnp.transpose(uz.reshape(b, n2, n1, 2 * e)[..., e:], (0, 2, 1, 3)).reshape(r * n2, e)

    rb = min(8, n1)
    assert n1 % rb == 0
    cb = min(1024, e)
    nkb = n1 // rb
    a_t = pl.pallas_call(
        functools.partial(_dft_c_kernel, rb=rb, cb=cb),
        grid=(r // rb, e // cb),
        in_specs=[pl.BlockSpec((rb, 2 * n2, cb), lambda i, c: (i, 0, c)),
                  pl.BlockSpec((rb, 2 * n2, 2 * n2), lambda i, c: (i % nkb, 0, 0)),
                  pl.BlockSpec((FNET_GROUP_W, FNET_GROUP_W), lambda i, c: (0, 0)),
                  pl.BlockSpec((FNET_GROUP_W, FNET_GROUP_W), lambda i, c: (0, 0)),
                  pl.BlockSpec((rb * n2, cb), lambda i, c: (i, c))],
        out_specs=pl.BlockSpec((rb * n2, cb), lambda i, c: (i, c)),
        out_shape=jax.ShapeDtypeStruct((r * n2, e), BF16),
        scratch_shapes=[pltpu.VMEM((rb * n2, cb), BF16), pltpu.VMEM((rb * n2, cb), BF16)],
        compiler_params=_params("parallel", "arbitrary"),
    )(a3, mtab, cc, sc, z_t)
    return jnp.transpose(a_t.reshape(b, n1, n2, e), (0, 2, 1, 3)).reshape(b * s, e)


def _nat_bias_pairs(rpb):
    qc = np.arange(GRID_W)[:, None]
    kc = np.arange(GRID_W)[None, :]
    ws = np.clip(qc - NAT_WIN_W // 2, 0, GRID_W - NAT_WIN_W)
    valid = (kc >= ws) & (kc < ws + NAT_WIN_W)
    rel = np.clip(kc - qc, -(NAT_WIN_W - 1), NAT_WIN_W - 1) + NAT_WIN_W - 1
    colb = jnp.where(jnp.asarray(valid)[None, None], rpb[:, :, jnp.asarray(rel)].astype(F32), NEG_MASK)
    return jnp.concatenate([colb[:, :-1], colb[:, 1:]], axis=-1) * LOG2_E


def _nat_kernel(*refs, n_rb, heads):
    q_ref = refs[0]
    k_refs = refs[1:1 + NAT_N_SUB]
    v_refs = refs[1 + NAT_N_SUB:1 + 2 * NAT_N_SUB]
    bias_ref, o_ref, s_ref, p_ref = refs[1 + 2 * NAT_N_SUB:]
    hd = NAT_HEAD_DIM
    w = GRID_W
    nq = NAT_Q_ROWS * w
    nks = NAT_KEY_SUB * w
    lane = lax.broadcasted_iota(jnp.int32, (w, 2 * w), 1)

    rows_half = NAT_Q_ROWS // 2
    hq = rows_half * w
    subs_half = NAT_N_SUB - 1
    pairs_sub = NAT_KEY_SUB // 2

    def run(window_starts):
        def scores(h, slot):
            off = pl.multiple_of(h * hd, hd)
            ks = [k_refs[j][:, :, pl.ds(off, hd)].reshape(nks, hd) for j in range(NAT_N_SUB)]
            for a in range(2):
                q = q_ref[a * rows_half:(a + 1) * rows_half, :, pl.ds(off, hd)].reshape(hq, hd)
                for j in range(a, a + subs_half):
                    s_ref[slot, a * hq:(a + 1) * hq, j * nks:(j + 1) * nks] = _dot_nt(q, ks[j])

        def softmax(h, slot):
            for i in range(NAT_Q_ROWS):
                lo = window_starts[i]
                kp_lo, kp_hi = lo // 2, (lo + NAT_WIN_H - 1) // 2
                a = i // rows_half
                assert a * pairs_sub <= kp_lo and kp_hi < (a + subs_half) * pairs_sub
                rows = slice(i * w, (i + 1) * w)
                tiles = []
                for kp in range(kp_lo, kp_hi + 1):
                    t = s_ref[slot, rows, kp * 2 * w:(kp + 1) * 2 * w] + bias_ref[h, 2 * kp - i + 3]
                    if 2 * kp < lo:
                        t = jnp.where(lane >= w, t, NEG_MASK)
                    if 2 * kp + 1 >= lo + NAT_WIN_H:
                        t = jnp.where(lane < w, t, NEG_MASK)
                    tiles.append(t)
                m = tiles[0]
                for t in tiles[1:]:
                    m = jnp.maximum(m, t)
                m = jnp.max(m, axis=-1, keepdims=True)
                es = [jnp.exp2(t - m) for t in tiles]
                tot = es[0]
                for ee in es[1:]:
                    tot = tot + ee
                inv = 1.0 / jnp.sum(tot, axis=-1, keepdims=True)
                for kp in range(a * pairs_sub, (a + subs_half) * pairs_sub):
                    cols = slice(kp * 2 * w, (kp + 1) * 2 * w)
                    if kp_lo <= kp <= kp_hi:
                        p_ref[slot, rows, cols] = (es[kp - kp_lo] * inv).astype(BF16)
                    else:
                        p_ref[slot, rows, cols] = jnp.zeros((w, 2 * w), BF16)

        def weighted_sum(h, slot):
            off = pl.multiple_of(h * hd, hd)
            vs = [v_refs[j][:, :, pl.ds(off, hd)].reshape(nks, hd) for j in range(NAT_N_SUB)]
            for a in range(2):
                acc = None
                for j in range(a, a + subs_half):
                    part = _dot(p_ref[slot, a * hq:(a + 1) * hq, j * nks:(j + 1) * nks], vs[j])
                    acc = part if acc is None else acc + part
                qrows = slice(a * rows_half, (a + 1) * rows_half)
                o_ref[qrows, :, pl.ds(off, hd)] = acc.astype(o_ref.dtype).reshape(rows_half, w, hd)

        def group_body(t, carry):
            hs = [NAT_HEAD_GROUP * t + g for g in range(NAT_HEAD_GROUP)]
            for g, h in enumerate(hs):
                scores(h, g)
            for g, h in enumerate(hs):
                softmax(h, g)
            for g, h in enumerate(hs):
                weighted_sum(h, g)
            return carry
        lax.fori_loop(0, heads // NAT_HEAD_GROUP, group_body, 0)

    rb = pl.program_id(2)
    half = NAT_WIN_H // 2
    interior = list(range(NAT_Q_ROWS))
    top = [max(i, half) for i in range(NAT_Q_ROWS)]
    bottom = [min(i, half) for i in range(NAT_Q_ROWS)]

    @pl.when(rb == 0)
    def _():
        run(top)

    @pl.when(rb == n_rb - 1)
    def _():
        run(bottom)

    @pl.when(jnp.logical_and(rb > 0, rb < n_rb - 1))
    def _():
        run(interior)


def _nat_mixer(qkvz, rpb, b, s):
    e = qkvz.shape[1] // 4
    w = GRID_W
    rows = s // w
    assert s % w == 0 and rows % NAT_Q_ROWS == 0 and rows >= 2 * NAT_Q_ROWS
    n_rb = rows // NAT_Q_ROWS
    hb = 8
    lw = hb * NAT_HEAD_DIM
    nhg = e // lw
    x4 = qkvz.reshape(b, rows, w, 4 * e)
    bias = _nat_bias_pairs(rpb)
    n_kblk = rows // NAT_KEY_SUB

    def kv_spec(j, sec):
        def imap(g, bi, r):
            blk = jnp.clip(2 * r - 1 + j, 0, n_kblk - 1)
            return (bi, blk, 0, sec * nhg + g)
        return pl.BlockSpec((None, NAT_KEY_SUB, w, lw), imap)

    in_specs = ([pl.BlockSpec((None, NAT_Q_ROWS, w, lw), lambda g, bi, r: (bi, r, 0, g))]
                + [kv_spec(j, 1) for j in range(NAT_N_SUB)]
                + [kv_spec(j, 2) for j in range(NAT_N_SUB)]
                + [pl.BlockSpec((hb, 2 * NAT_WIN_H - 2, w, 2 * w), lambda g, bi, r: (g, 0, 0, 0))])
    nq = NAT_Q_ROWS * w
    nk = NAT_N_SUB * NAT_KEY_SUB * w
    out = pl.pallas_call(
        functools.partial(_nat_kernel, n_rb=n_rb, heads=hb),
        grid=(nhg, b, n_rb),
        in_specs=in_specs,
        out_specs=pl.BlockSpec((None, NAT_Q_ROWS, w, lw), lambda g, bi, r: (bi, r, 0, g)),
        out_shape=jax.ShapeDtypeStruct((b, rows, w, e), BF16),
        scratch_shapes=[pltpu.VMEM((NAT_HEAD_GROUP, nq, nk), F32),
                        pltpu.VMEM((NAT_HEAD_GROUP, nq, nk), BF16)],
        compiler_params=_params("parallel", "parallel", "arbitrary"),
    )(*([x4] * (1 + 2 * NAT_N_SUB)), bias)
    return out.reshape(b * s, e)


def _log_gate(g1, wa2_ref, ba_ref):
    g_hi, g_lo = _split_bf16(g1)
    w_hi = wa2_ref[0]
    w_lo = wa2_ref[1]
    pre = _dot(g_hi, w_hi) + _dot(g_lo, w_hi) + _dot(g_hi, w_lo) + ba_ref[...]
    return (jnp.minimum(pre, 0.0) - jnp.log(1.0 + jnp.exp(-jnp.abs(pre)))) / GLA_GATE_TEMP


def _gla_superchunk(q, k, v, la, state_ref, reverse):
    n, dk = q.shape
    c = GLA_CHUNK
    nb = n // c
    ii = lax.broadcasted_iota(jnp.int32, (n, n), 0)
    jj = lax.broadcasted_iota(jnp.int32, (n, n), 1)
    tri = jnp.where((jj >= ii) if reverse else (jj <= ii), 1.0, 0.0).astype(BF16)
    la_hi, la_lo = _split_bf16(la)
    bc = _dot(tri, la_hi) + _dot(tri, la_lo)
    qf = q.astype(F32) * (dk ** -0.5)
    kf = k.astype(F32)

    def brow(idx):
        return bc[idx:idx + 1]

    ka_parts = []
    mids = []
    for blk in range(nb):
        r0 = blk * c
        b_mid = brow(r0 + c // 2) if reverse else brow(r0 + c // 2 - 1)
        mids.append(b_mid)
        ka_parts.append((kf[r0:r0 + c] * jnp.exp(b_mid - bc[r0:r0 + c])).astype(BF16))
    ka = jnp.concatenate(ka_parts, axis=0)

    row_i = lax.broadcasted_iota(jnp.int32, (c, n), 0)
    col_j = lax.broadcasted_iota(jnp.int32, (c, n), 1)
    s_rows = []
    for blk in range(nb):
        r0 = blk * c
        b_blk = bc[r0:r0 + c]
        q_blk = qf[r0:r0 + c]
        qa = (q_blk * jnp.exp(b_blk - mids[blk])).astype(BF16)
        s_diag = _dot_nt(qa, ka)
        in_blk = jnp.logical_and(col_j >= r0, col_j < r0 + c)
        if reverse:
            keep = jnp.logical_and(in_blk, col_j > row_i + r0)
        else:
            keep = jnp.logical_and(in_blk, col_j <= row_i + r0)
        s_blk = jnp.where(keep, s_diag, 0.0)
        has_other = blk < nb - 1 if reverse else blk > 0
        if has_other:
            b_s = brow(r0 + c) if reverse else brow(r0 - 1)
            qo = (q_blk * jnp.exp(b_blk - b_s)).astype(BF16)
            if reverse:
                ko = (kf[r0 + c:] * jnp.exp(b_s - bc[r0 + c:])).astype(BF16)
                ko = jnp.concatenate([jnp.zeros((r0 + c, dk), BF16), ko], axis=0)
            else:
                ko = (kf[:r0] * jnp.exp(b_s - bc[:r0])).astype(BF16)
                ko = jnp.concatenate([ko, jnp.zeros((n - r0, dk), BF16)], axis=0)
            s_blk = s_blk + _dot_nt(qo, ko)
        s_rows.append(s_blk.astype(BF16))
    scores = jnp.concatenate(s_rows, axis=0)

    b_end = brow(0) if reverse else brow(n - 1)
    q_in = (qf * jnp.exp(bc)).astype(BF16)
    o = _dot(scores, v) + _dot(q_in, state_ref[...].astype(BF16))
    k_out = (kf * jnp.exp(b_end - bc)).astype(BF16)
    ones = jnp.ones((n, 128), BF16)
    decay = jnp.exp(_dot_tn(la_hi, ones) + _dot_tn(la_lo, ones))
    upd = _dot_tn(k_out, v)
    for lb in range(v.shape[1] // 128):
        sl = slice(lb * 128, (lb + 1) * 128)
        state_ref[:, sl] = decay * state_ref[:, sl] + upd[:, sl]
    return o


def _gla_fwd_kernel(q_ref, k_ref, v_ref, g1_ref, wa2_ref, ba_ref, o_ref, state_ref):
    @pl.when(pl.program_id(2) == 0)
    def _():
        state_ref[...] = jnp.zeros_like(state_ref)

    n_heads, dk, dv = state_ref.shape
    la = _log_gate(g1_ref[:, :GLA_GATE_RANK], wa2_ref, ba_ref)
    n = GLA_SUPER
    for sc in range(q_ref.shape[0] // n):
        rows = slice(sc * n, (sc + 1) * n)
        for g in range(n_heads):
            kl = slice(g * dk, (g + 1) * dk)
            vl = slice(g * dv, (g + 1) * dv)
            o = _gla_superchunk(q_ref[rows, kl], k_ref[rows, kl], v_ref[rows, vl], la[rows, kl],
                                state_ref.at[g], False)
            o_ref[rows, vl] = o.astype(o_ref.dtype)


def _gla_bwd_kernel(q_ref, k_ref, v_ref, g1_ref, wa2_ref, ba_ref, of_ref, o_ref, state_ref):
    @pl.when(pl.program_id(2) == 0)
    def _():
        state_ref[...] = jnp.zeros_like(state_ref)

    n_heads, dk, dv = state_ref.shape
    la = _log_gate(g1_ref[:, GLA_GATE_RANK:], wa2_ref, ba_ref)
    n = GLA_SUPER
    for sc in reversed(range(q_ref.shape[0] // n)):
        rows = slice(sc * n, (sc + 1) * n)
        for g in range(n_heads):
            kl = slice(g * dk, (g + 1) * dk)
            vl = slice(g * dv, (g + 1) * dv)
            o = _gla_superchunk(q_ref[rows, kl], k_ref[rows, kl], v_ref[rows, vl], la[rows, kl],
                                state_ref.at[g], True)
            o_ref[rows, vl] = (o + of_ref[rows, vl].astype(F32)).astype(o_ref.dtype)


def _gla_mixer(qkvz, g1, wa2_f, ba_f, wa2_b, ba_b, g_norm, b, s):
    hh = GLA_HEADS
    dv = g_norm.shape[0]
    e = hh * dv
    kd = (qkvz.shape[1] - 2 * e) // 2
    dk = kd // hh
    hp = GLA_HEADS_PER_STEP
    wk, wv = hp * dk, hp * dv
    assert hh % hp == 0 and kd % wk == 0 and (2 * kd) % wv == 0
    tb = min(1024, s)
    assert s % tb == 0 and tb % GLA_SUPER == 0
    nt = s // tb
    x3 = qkvz.reshape(b, s, qkvz.shape[1])
    g3 = g1.reshape(b, s, 2 * GLA_GATE_RANK)
    k0 = kd // wk
    v0 = 2 * kd // wv

    def specs(tmap):
        return [pl.BlockSpec((None, tb, wk), lambda bi, h, t: (bi, tmap(t), h)),
                pl.BlockSpec((None, tb, wk), lambda bi, h, t: (bi, tmap(t), k0 + h)),
                pl.BlockSpec((None, tb, wv), lambda bi, h, t: (bi, tmap(t), v0 + h)),
                pl.BlockSpec((None, tb, 2 * GLA_GATE_RANK), lambda bi, h, t: (bi, tmap(t), 0)),
                pl.BlockSpec((2, GLA_GATE_RANK, wk), lambda bi, h, t: (0, 0, h)),
                pl.BlockSpec((1, wk), lambda bi, h, t: (0, h))]

    def gate_w(wa2):
        hi, lo = _split_bf16(wa2)
        return jnp.stack([hi, lo])

    fwd = lambda t: t
    o_f = pl.pallas_call(
        _gla_fwd_kernel,
        grid=(b, hh // hp, nt),
        in_specs=specs(fwd),
        out_specs=pl.BlockSpec((None, tb, wv), lambda bi, h, t: (bi, t, h)),
        out_shape=jax.ShapeDtypeStruct((b, s, e), BF16),
        scratch_shapes=[pltpu.VMEM((hp, dk, dv), F32)],
        compiler_params=_params("parallel", "parallel", "arbitrary"),
    )(x3, x3, x3, g3, gate_w(wa2_f), ba_f.reshape(1, kd))

    rev = lambda t: nt - 1 - t
    out = pl.pallas_call(
        _gla_bwd_kernel,
        grid=(b, hh // hp, nt),
        in_specs=specs(rev) + [pl.BlockSpec((None, tb, wv), lambda bi, h, t: (bi, rev(t), h))],
        out_specs=pl.BlockSpec((None, tb, wv), lambda bi, h, t: (bi, rev(t), h)),
        out_shape=jax.ShapeDtypeStruct((b, s, e), BF16),
        scratch_shapes=[pltpu.VMEM((hp, dk, dv), F32)],
        compiler_params=_params("parallel", "parallel", "arbitrary"),
    )(x3, x3, x3, g3, gate_w(wa2_b), ba_b.reshape(1, kd), o_f)
    return out.reshape(b * s, e)


def _fnet_layer(x, g_pre, g_post, w_in, w_out):
    b, s, d = x.shape
    x2 = x.reshape(b * s, d)
    uz = _in_proj(x2, g_pre, w_in.astype(BF16))
    a = _fnet_mixer(uz, b, s)
    return _out_proj(a, w_out.astype(BF16), g_post, x2).reshape(b, s, d)


def _nat_layer(x, g_pre, g_post, w_in, rpb, w_out):
    b, s, d = x.shape
    x2 = x.reshape(b * s, d)
    e = w_in.shape[1] // 4
    q_cols = (jnp.arange(w_in.shape[1]) < e)[None, :]
    w_scaled = jnp.where(q_cols, w_in * (NAT_HEAD_DIM ** -0.5 * LOG2_E), w_in)
    qkvz = _in_proj(x2, g_pre, w_scaled.astype(BF16))
    m = _nat_mixer(qkvz, rpb, b, s)
    return _gated_out_proj(m, qkvz, 3 * e, w_out.astype(BF16), g_post, x2).reshape(b, s, d)


def _gla_layer(x, g_pre, g_post, w_in, wa1_f, wa2_f, ba_f, wa1_b, wa2_b, ba_b, g_norm, w_out):
    b, s, d = x.shape
    x2 = x.reshape(b * s, d)
    qkvz, g1 = _in_proj(x2, g_pre, w_in.astype(BF16), jnp.concatenate([wa1_f, wa1_b], axis=1))
    m = _gla_mixer(qkvz, g1, wa2_f, ba_f, wa2_b, ba_b, g_norm, b, s)
    z_start = qkvz.shape[1] - m.shape[1]
    return _gated_out_proj(m, qkvz, z_start, w_out.astype(BF16), g_post, x2,
                           head_gain=g_norm).reshape(b, s, d)


def _trunk(x, norm_pre_g, norm_post_g, fnet_w_in, fnet_w_out, nat_w_in, nat_rpb, nat_w_out,
           gla_w_in, gla_wa1_f, gla_wa2_f, gla_ba_f, gla_wa1_b, gla_wa2_b, gla_ba_b, gla_g_norm,
           gla_w_out):
    depth = norm_pre_g.shape[0]
    for i in range(depth):
        m, j = i % 3, i // 3
        if m == 0:
            x = _fnet_layer(x, norm_pre_g[i], norm_post_g[i], fnet_w_in[j], fnet_w_out[j])
        elif m == 1:
            x = _nat_layer(x, norm_pre_g[i], norm_post_g[i], nat_w_in[j], nat_rpb[j], nat_w_out[j])
        else:
            x = _gla_layer(x, norm_pre_g[i], norm_post_g[i], gla_w_in[j], gla_wa1_f[j], gla_wa2_f[j],
                           gla_ba_f[j], gla_wa1_b[j], gla_wa2_b[j], gla_ba_b[j], gla_g_norm[j],
                           gla_w_out[j])
    return x


def kernel(x_prompt, x_sample, norm_pre_g, norm_post_g, fnet_w_in, fnet_w_out, nat_w_in, nat_rpb,
           nat_w_out, gla_w_in, gla_wa1_f, gla_wa2_f, gla_ba_f, gla_wa1_b, gla_wa2_b, gla_ba_b,
           gla_g_norm, gla_w_out):
    params = (norm_pre_g, norm_post_g, fnet_w_in, fnet_w_out, nat_w_in, nat_rpb, nat_w_out,
              gla_w_in, gla_wa1_f, gla_wa2_f, gla_ba_f, gla_wa1_b, gla_wa2_b, gla_ba_b, gla_g_norm,
              gla_w_out)
    return (_trunk(x_prompt, *params), _trunk(x_sample, *params))
```
